```python
import jax, jax.numpy as jnp
from jax import lax
import numpy as np

D_MODEL = 1024
BATCH = 4
SEQ = 8192
DEPTH = 1
DEC_BATCH = 8
DEC_SEQ = 32
PAST_LEN = 1024

CHUNK = 64
GLA_HEADS = 4
GLA_DK = 64
GLA_DV = 128
GLA_GATE_RANK = 16
GLA_GATE_NORM = 16.0
HG_HEADS = 4
HG_EXPAND = 128
HG_HEAD_DIM = 128
D_FF = 2816
EPS = 1e-6
D_IN = 2 * GLA_HEADS * GLA_DK + 2 * GLA_HEADS * GLA_DV + GLA_GATE_RANK + 2 * HG_HEADS * HG_EXPAND + 2 * HG_HEADS * HG_HEAD_DIM
D_MIX = GLA_HEADS * GLA_DV + HG_HEADS * HG_HEAD_DIM

kernel_name = "hybrid_gla_hgrn2_macaron_stream_step"


def _split_indices():
    sizes = (GLA_HEADS * GLA_DK, GLA_HEADS * GLA_DK, GLA_HEADS * GLA_DV, GLA_HEADS * GLA_DV, GLA_GATE_RANK,
             HG_HEADS * HG_EXPAND, HG_HEADS * HG_EXPAND, HG_HEADS * HG_HEAD_DIM, HG_HEADS * HG_HEAD_DIM)
    idx, acc = [], 0
    for s in sizes[:-1]:
        acc += s
        idx.append(acc)
    return idx


def _rmsnorm(x, g):
    xf = x.astype(jnp.float32)
    y = xf * lax.rsqrt(jnp.mean(xf * xf, axis=-1, keepdims=True) + EPS) * g.astype(jnp.float32)
    return y.astype(x.dtype)


def _swiglu(h, w_gate, w_up, w_down):
    return (jax.nn.silu(h @ w_gate) * (h @ w_up)) @ w_down


def _gated_head_norm(o, g, gate):
    o = o * lax.rsqrt(jnp.mean(o * o, axis=-1, keepdims=True) + EPS) * g.astype(jnp.float32)
    return o * jax.nn.silu(gate.astype(jnp.float32))


def _chunked_gated_recurrence(q, k, v, log_a, s0, L):
    B, T, H, dk = q.shape
    dv = v.shape[-1]
    N = T // L

    def to_blocks(a):
        return a.astype(jnp.float32).reshape(B, N, L, H, a.shape[-1]).transpose(1, 0, 3, 2, 4)

    causal = jnp.tril(jnp.ones((L, L), dtype=bool))[:, :, None]

    def step(S, blk):
        qc, kc, vc, gc = blk
        b = jnp.cumsum(gc, axis=-2)
        decay = jnp.exp(jnp.where(causal, b[..., :, None, :] - b[..., None, :, :], -jnp.inf))
        scores = jnp.einsum('bhid,bhijd,bhjd->bhij', qc, decay, kc)
        o = jnp.einsum('bhid,bhde->bhie', qc * jnp.exp(b), S) + jnp.einsum('bhij,bhje->bhie', scores, vc)
        b_last = b[..., -1:, :]
        S = S * jnp.exp(b_last[..., 0, :])[..., None] + jnp.einsum('bhjd,bhje->bhde', kc * jnp.exp(b_last - b), vc)
        return S, o

    S, o = lax.scan(step, s0.astype(jnp.float32), (to_blocks(q), to_blocks(k), to_blocks(v), to_blocks(log_a)))
    o = o.transpose(1, 0, 3, 2, 4).reshape(B, T, H, dv)
    return o, S


def _mixer(h, w_in, gla_gate_up, gla_gate_bias, gla_onorm, hg_onorm, lb, s_gla, s_hg, L):
    B, T, _ = h.shape
    proj = h @ w_in
    gq, gk, gv, gr, gd, hq, hf, hi, hgate = jnp.split(proj, _split_indices(), axis=-1)
    gla_log_a = jax.nn.log_sigmoid((gd @ gla_gate_up + gla_gate_bias).astype(jnp.float32)) / GLA_GATE_NORM
    q = gq.reshape(B, T, GLA_HEADS, GLA_DK) * (GLA_DK ** -0.5)
    k = gk.reshape(B, T, GLA_HEADS, GLA_DK)
    v = gv.reshape(B, T, GLA_HEADS, GLA_DV)
    o_gla, s_gla_new = _chunked_gated_recurrence(q, k, v, gla_log_a.reshape(B, T, GLA_HEADS, GLA_DK), s_gla, L)
    o_gla = _gated_head_norm(o_gla, gla_onorm, gr.reshape(B, T, GLA_HEADS, GLA_DV))
    z = hf.astype(jnp.float32).reshape(B, T, HG_HEADS, HG_EXPAND)
    lb = lb.reshape(HG_HEADS, HG_EXPAND)
    f = lb + (1.0 - lb) * jax.nn.sigmoid(z)
    hk = (1.0 - lb) * jax.nn.sigmoid(-z)
    hqs = jax.nn.silu(hq.astype(jnp.float32)).reshape(B, T, HG_HEADS, HG_EXPAND)
    hv = hi.reshape(B, T, HG_HEADS, HG_HEAD_DIM)
    o_hg, s_hg_new = _chunked_gated_recurrence(hqs, hk, hv, jnp.log(f), s_hg, L)
    o_hg = _gated_head_norm(o_hg, hg_onorm, hgate.reshape(B, T, HG_HEADS, HG_HEAD_DIM))
    o = jnp.concatenate([o_gla.reshape(B, T, -1), o_hg.reshape(B, T, -1)], axis=-1)
    return o.astype(h.dtype), s_gla_new, s_hg_new


def _trunk(x, s_gla, s_hg, L, params, lb_all):
    (norm_ffn1, ffn1_w_gate, ffn1_w_up, ffn1_w_down, norm_mix, w_in, gla_gate_up, gla_gate_bias,
     gla_onorm, hg_onorm, w_out, norm_ffn2, ffn2_w_gate, ffn2_w_up, ffn2_w_down, norm_final) = params
    new_gla, new_hg = [], []
    for l in range(DEPTH):
        x = x + 0.5 * _swiglu(_rmsnorm(x, norm_ffn1[l]), ffn1_w_gate[l], ffn1_w_up[l], ffn1_w_down[l])
        mix, sg, sh = _mixer(_rmsnorm(x, norm_mix[l]), w_in[l], gla_gate_up[l], gla_gate_bias[l],
                             gla_onorm[l], hg_onorm[l], lb_all[l], s_gla[l], s_hg[l], L)
        x = x + mix @ w_out[l]
        x = x + 0.5 * _swiglu(_rmsnorm(x, norm_ffn2[l]), ffn2_w_gate[l], ffn2_w_up[l], ffn2_w_down[l])
        new_gla.append(sg)
        new_hg.append(sh)
    return _rmsnorm(x, norm_final), jnp.stack(new_gla), jnp.stack(new_hg)


def setup_inputs(seed: int = 0) -> dict:
    key = jax.random.key(seed)
    ks = jax.random.split(key, 24)
    f32 = jnp.float32
    nrm = lambda k, shape, s: jax.random.normal(k, shape, f32) * s
    gain = lambda k, shape: 1.0 + 0.01 * jax.random.normal(k, shape, f32)
    return {
        "x_prompt": nrm(ks[0], (BATCH, SEQ, D_MODEL), 1.0),
        "x_sample": nrm(ks[1], (DEC_BATCH, DEC_SEQ, D_MODEL), 1.0),
        "state_gla": nrm(ks[2], (DEPTH, DEC_BATCH, GLA_HEADS, GLA_DK, GLA_DV), 1.0),
        "state_hgrn": nrm(ks[3], (DEPTH, DEC_BATCH, HG_HEADS, HG_EXPAND, HG_HEAD_DIM), 1.0),
        "norm_ffn1": gain(ks[4], (DEPTH, D_MODEL)),
        "ffn1_w_gate": nrm(ks[5], (DEPTH, D_MODEL, D_FF), D_MODEL ** -0.5),
        "ffn1_w_up": nrm(ks[6], (DEPTH, D_MODEL, D_FF), D_MODEL ** -0.5),
        "ffn1_w_down": nrm(ks[7], (DEPTH, D_FF, D_MODEL), D_FF ** -0.5),
        "norm_mix": gain(ks[8], (DEPTH, D_MODEL)),
        "w_in": nrm(ks[9], (DEPTH, D_MODEL, D_IN), D_MODEL ** -0.5),
        "gla_gate_up": nrm(ks[10], (DEPTH, GLA_GATE_RANK, GLA_HEADS * GLA_DK), GLA_GATE_RANK ** -0.5),
        "gla_gate_bias": nrm(ks[11], (DEPTH, GLA_HEADS * GLA_DK), 0.01),
        "gla_onorm": gain(ks[12], (DEPTH, GLA_DV)),
        "hg_lower_bound_logits": nrm(ks[13], (DEPTH + 1, HG_HEADS * HG_EXPAND), 0.1),
        "hg_onorm": gain(ks[14], (DEPTH, HG_HEAD_DIM)),
        "w_out": nrm(ks[15], (DEPTH, D_MIX, D_MODEL), D_MIX ** -0.5),
        "norm_ffn2": gain(ks[16], (DEPTH, D_MODEL)),
        "ffn2_w_gate": nrm(ks[17], (DEPTH, D_MODEL, D_FF), D_MODEL ** -0.5),
        "ffn2_w_up": nrm(ks[18], (DEPTH, D_MODEL, D_FF), D_MODEL ** -0.5),
        "ffn2_w_down": nrm(ks[19], (DEPTH, D_FF, D_MODEL), D_FF ** -0.5),
        "norm_final": gain(ks[20], (D_MODEL,)),
    }


def reference(x_prompt, x_sample, state_gla, state_hgrn, norm_ffn1, ffn1_w_gate, ffn1_w_up, ffn1_w_down,
              norm_mix, w_in, gla_gate_up, gla_gate_bias, gla_onorm, hg_lower_bound_logits, hg_onorm, w_out,
              norm_ffn2, ffn2_w_gate, ffn2_w_up, ffn2_w_down, norm_final):
    params = (norm_ffn1, ffn1_w_gate, ffn1_w_up, ffn1_w_down, norm_mix, w_in, gla_gate_up, gla_gate_bias,
              gla_onorm, hg_onorm, w_out, norm_ffn2, ffn2_w_gate, ffn2_w_up, ffn2_w_down, norm_final)
    lb_all = jnp.cumsum(jax.nn.softmax(hg_lower_bound_logits.astype(jnp.float32), axis=0), axis=0)
    Bp = x_prompt.shape[0]
    zero_gla = jnp.zeros((DEPTH, Bp, GLA_HEADS, GLA_DK, GLA_DV), jnp.float32)
    zero_hg = jnp.zeros((DEPTH, Bp, HG_HEADS, HG_EXPAND, HG_HEAD_DIM), jnp.float32)
    L_prompt = min(CHUNK, x_prompt.shape[1])
    y_prompt, gla_p, hg_p = _trunk(x_prompt, zero_gla, zero_hg, L_prompt, params, lb_all)
    y_sample, gla_s, hg_s = _trunk(x_sample, state_gla, state_hgrn, x_sample.shape[1], params, lb_all)
    sd_g, sd_h = state_gla.dtype, state_hgrn.dtype
    return (y_prompt, y_sample, gla_p.astype(sd_g), hg_p.astype(sd_h), gla_s.astype(sd_g), hg_s.astype(sd_h))
```

```python
import functools

import jax
import jax.numpy as jnp
from jax import lax
from jax.experimental import pallas as pl
from jax.experimental.pallas import tpu as pltpu

D_MODEL = 1024
CHUNK = 64
GLA_HEADS = 4
GLA_DK = 64
GLA_DV = 128
GLA_GATE_RANK = 16
GLA_GATE_NORM = 16.0
HG_HEADS = 4
HG_EXPAND = 128
HG_HEAD_DIM = 128
D_FF = 2816
EPS = 1e-6

LANES = 128
N_HEADS = GLA_HEADS + HG_HEADS
FF_CHUNK = 256
N_FF_CHUNKS = D_FF // FF_CHUNK
VMEM_LIMIT = 56 * 1024 * 1024

_GQ, _GK, _GV, _GR = 0, 256, 512, 1024
_HQ, _HF, _HI, _HGATE, _GD = 1536, 2048, 2560, 3072, 3584
D_PROJ = _GD + LANES

_F32 = jnp.float32
_BF16 = jnp.bfloat16


def _rmsnorm(x, g):
    return x * lax.rsqrt(jnp.mean(x * x, axis=-1, keepdims=True) + EPS) * g


def _sigmoid(x):
    return 1.0 / (1.0 + jnp.exp(-x))


def _dot(a, b):
    return jnp.dot(a, b, preferred_element_type=_F32)


def _dot_nt(a, b):
    return lax.dot_general(a, b, (((1,), (1,)), ((), ())), preferred_element_type=_F32)


def _dot_tn(a, b):
    return lax.dot_general(a, b, (((0,), (0,)), ((), ())), preferred_element_type=_F32)


def _ffn_kernel(x_ref, g_ref, wg_ref, wu_ref, wd_ref, gf_ref, o_ref, acc_ref, *, final_norm):
    x = x_ref[...]
    h = _rmsnorm(x, g_ref[...]).astype(_BF16)
    for c in range(N_FF_CHUNKS):
        gate = _dot(h, wg_ref[c])
        up = _dot(h, wu_ref[c])
        a = (gate * _sigmoid(gate) * up).astype(_BF16)
        d = _dot(a, wd_ref[c])
        if c == 0:
            acc_ref[...] = d
        else:
            acc_ref[...] += d
    y = x + 0.5 * acc_ref[...]
    if final_norm:
        y = _rmsnorm(y, gf_ref[...])
    o_ref[...] = y


def _const_spec(shape):
    zeros = (0,) * len(shape)
    return pl.BlockSpec(shape, lambda *_: zeros, pipeline_mode=pl.Buffered(1))


def _ffn(x2d, g, wg, wu, wd, gf, *, tm, final_norm):
    n = x2d.shape[0]
    assert n % tm == 0
    row_spec = pl.BlockSpec((tm, D_MODEL), lambda i: (i, 0))
    return pl.pallas_call(
        functools.partial(_ffn_kernel, final_norm=final_norm),
        grid=(n // tm,),
        in_specs=[row_spec, _const_spec(g.shape), _const_spec(wg.shape), _const_spec(wu.shape),
                  _const_spec(wd.shape), _const_spec(gf.shape)],
        out_specs=row_spec,
        out_shape=jax.ShapeDtypeStruct((n, D_MODEL), _F32),
        scratch_shapes=[pltpu.VMEM((tm, D_MODEL), _F32)],
        compiler_params=pltpu.CompilerParams(dimension_semantics=("arbitrary",),
                                             vmem_limit_bytes=VMEM_LIMIT),
        name="ffn_final" if final_norm else "ffn",
    )(x2d, g, wg, wu, wd, gf)


def _mixer_kernel(x_ref, st_in_ref, g_ref, win_ref, gup_ref, gbias_ref, lb_ref, gon_ref, hon_ref, wout_ref,
                  o_ref, st_ref, proj_ref, mix_ref, *, tm, blk):
    t = pl.program_id(1)

    @pl.when(t == 0)
    def _():
        for h in range(N_HEADS):
            st_ref[h] = st_in_ref[h].T

    x = x_ref[...]
    h_in = _rmsnorm(x, g_ref[...]).astype(_BF16)
    for c0 in range(0, D_PROJ, 512):
        c1 = min(c0 + 512, D_PROJ)
        proj_ref[:, c0:c1] = _dot(h_in, win_ref[:, c0:c1])

    row = lax.broadcasted_iota(jnp.int32, (blk, blk), 0)
    col = lax.broadcasted_iota(jnp.int32, (blk, blk), 1)
    causal = col <= row
    tri = causal.astype(_BF16)
    lane = lax.broadcasted_iota(jnp.int32, (blk, LANES), 1)
    lb = lb_ref[...]
    gon = gon_ref[...]
    hon = hon_ref[...]

    def cumsum_rows(g):
        hi = g.astype(_BF16)
        r1 = g - hi.astype(_F32)
        mid = r1.astype(_BF16)
        lo = (r1 - mid.astype(_F32)).astype(_BF16)
        return _dot(tri, hi) + _dot(tri, mid) + _dot(tri, lo)

    def head_step(idx, rows, qt, kt, v, e_last, onorm, gate_col):
        s_t = st_ref[idx]
        scores = jnp.where(causal, _dot_nt(qt, kt), 0.0).astype(_BF16)
        o = _dot_nt(qt, s_t.astype(_BF16)) + _dot(scores, v)
        st_ref[idx] = (s_t + _dot_tn(v, kt)) * e_last
        gate = proj_ref[rows, gate_col:gate_col + LANES]
        o = o * lax.rsqrt(jnp.mean(o * o, axis=-1, keepdims=True) + EPS) * onorm
        o = o * (gate * _sigmoid(gate))
        mix_ref[rows, idx * LANES:(idx + 1) * LANES] = o.astype(_BF16)

    def block_step(c, carry):
        rows = pl.ds(pl.multiple_of(c * blk, blk), blk)
        gd = proj_ref[rows, _GD:_GD + LANES].astype(_BF16)
        logits = _dot(gd, gup_ref[...]) + gbias_ref[...]
        g_gla = (jnp.minimum(logits, 0.0) - jnp.log1p(jnp.exp(-jnp.abs(logits)))) / GLA_GATE_NORM
        b_gla = cumsum_rows(g_gla)
        z = proj_ref[rows, _HF:_HF + HG_HEADS * HG_EXPAND]
        f = lb + (1.0 - lb) * _sigmoid(z)
        hk = (1.0 - lb) * _sigmoid(-z)
        b_hg = cumsum_rows(jnp.log(f))

        for p in range(GLA_HEADS // 2):
            sl = slice(p * LANES, (p + 1) * LANES)
            b = b_gla[:, sl]
            eb = jnp.exp(b)
            q = proj_ref[rows, _GQ + p * LANES:_GQ + (p + 1) * LANES] * (GLA_DK ** -0.5) * eb
            k = proj_ref[rows, _GK + p * LANES:_GK + (p + 1) * LANES] * jnp.exp(-b)
            e_last = eb[blk - 1:blk, :]
            for hh in range(2):
                idx = 2 * p + hh
                m = (lane >= hh * GLA_DK) & (lane < (hh + 1) * GLA_DK)
                qt = jnp.where(m, q, 0.0).astype(_BF16)
                kt = jnp.where(m, k, 0.0).astype(_BF16)
                v = proj_ref[rows, _GV + idx * GLA_DV:_GV + (idx + 1) * GLA_DV].astype(_BF16)
                head_step(idx, rows, qt, kt, v, e_last, gon, _GR + idx * GLA_DV)

        for hd in range(HG_HEADS):
            sl = slice(hd * LANES, (hd + 1) * LANES)
            b = b_hg[:, sl]
            eb = jnp.exp(b)
            hq = proj_ref[rows, _HQ + hd * LANES:_HQ + (hd + 1) * LANES]
            qt = (hq * _sigmoid(hq) * eb).astype(_BF16)
            kt = (hk[:, sl] * jnp.exp(-b)).astype(_BF16)
            v = proj_ref[rows, _HI + hd * LANES:_HI + (hd + 1) * LANES].astype(_BF16)
            head_step(GLA_HEADS + hd, rows, qt, kt, v, eb[blk - 1:blk, :], hon, _HGATE + hd * LANES)
        return carry

    lax.fori_loop(0, tm // blk, block_step, 0)

    o_ref[...] = x + _dot(mix_ref[...], wout_ref[...])

    @pl.when(t == pl.num_programs(1) - 1)
    def _():
        for h in range(N_HEADS):
            st_ref[h] = st_ref[h].T


def _mixer(x, st_in, g, win, gup, gbias, lb, gon, hon, wout, *, tm, blk):
    bsz, seq, _ = x.shape
    assert seq % tm == 0 and tm % blk == 0
    row_spec = pl.BlockSpec((None, tm, D_MODEL), lambda b, t: (b, t, 0))
    st_spec = pl.BlockSpec((None, N_HEADS, LANES, LANES), lambda b, t: (b, 0, 0, 0))
    return pl.pallas_call(
        functools.partial(_mixer_kernel, tm=tm, blk=blk),
        grid=(bsz, seq // tm),
        in_specs=[row_spec, st_spec] + [_const_spec(a.shape) for a in (g, win, gup, gbias, lb, gon, hon, wout)],
        out_specs=[row_spec, st_spec],
        out_shape=[jax.ShapeDtypeStruct(x.shape, _F32),
                   jax.ShapeDtypeStruct((bsz, N_HEADS, LANES, LANES), _F32)],
        scratch_shapes=[pltpu.VMEM((tm, D_PROJ), _F32), pltpu.VMEM((tm, D_MODEL), _BF16)],
        compiler_params=pltpu.CompilerParams(dimension_semantics=("arbitrary", "arbitrary"),
                                             vmem_limit_bytes=VMEM_LIMIT),
        name="mixer",
    )(x, st_in, g, win, gup, gbias, lb, gon, hon, wout)


def _pack_state(s_gla, s_hg):
    bsz = s_gla.shape[0]
    zeros = jnp.zeros_like(s_gla)
    even = jnp.concatenate([s_gla, zeros], axis=2)
    odd = jnp.concatenate([zeros, s_gla], axis=2)
    is_odd = (jnp.arange(GLA_HEADS) % 2 == 1)[None, :, None, None]
    gla = jnp.where(is_odd, odd, even)
    return jnp.concatenate([gla, s_hg], axis=1).astype(_F32)


def _unpack_state(st):
    gla = st[:, :GLA_HEADS]
    is_odd = (jnp.arange(GLA_HEADS) % 2 == 1)[None, :, None, None]
    gla = jnp.where(is_odd, gla[:, :, GLA_DK:], gla[:, :, :GLA_DK])
    return gla, st[:, GLA_HEADS:]


def _ff_chunks_cols(w):
    return w.reshape(w.shape[0], N_FF_CHUNKS, FF_CHUNK).transpose(1, 0, 2).astype(_BF16)


def _ff_chunks_rows(w):
    return w.reshape(N_FF_CHUNKS, FF_CHUNK, w.shape[1]).astype(_BF16)


def kernel(x_prompt, x_sample, state_gla, state_hgrn, norm_ffn1, ffn1_w_gate, ffn1_w_up, ffn1_w_down, norm_mix, w_in, gla_gate_up, gla_gate_bias, gla_onorm, hg_lower_bound_logits, hg_onorm, w_out, norm_ffn2, ffn2_w_gate, ffn2_w_up, ffn2_w_down, norm_final):
    assert norm_ffn1.shape[0] == 1, "single layer"
    row = lambda v: v.reshape(1, -1).astype(_F32)
    f1 = (row(norm_ffn1[0]), _ff_chunks_cols(ffn1_w_gate[0]), _ff_chunks_cols(ffn1_w_up[0]),
          _ff_chunks_rows(ffn1_w_down[0]), row(norm_final))
    f2 = (row(norm_ffn2[0]), _ff_chunks_cols(ffn2_w_gate[0]), _ff_chunks_cols(ffn2_w_up[0]),
          _ff_chunks_rows(ffn2_w_down[0]), row(norm_final))
    wi = w_in[0]
    o_gd = 2 * GLA_HEADS * GLA_DK + 2 * GLA_HEADS * GLA_DV
    win = jnp.concatenate(
        [wi[:, :o_gd], wi[:, o_gd + GLA_GATE_RANK:], wi[:, o_gd:o_gd + GLA_GATE_RANK],
         jnp.zeros((D_MODEL, LANES - GLA_GATE_RANK), wi.dtype)], axis=1).astype(_BF16)
    gup = jnp.concatenate([gla_gate_up[0], jnp.zeros((LANES - GLA_GATE_RANK, GLA_HEADS * GLA_DK), _F32)],
                          axis=0).astype(_BF16)
    lb = jnp.cumsum(jax.nn.softmax(hg_lower_bound_logits.astype(_F32), axis=0), axis=0)[0]
    mix_w = (row(norm_mix[0]), win, gup, row(gla_gate_bias[0]), row(lb), row(gla_onorm[0]), row(hg_onorm[0]),
             w_out[0].astype(_BF16))

    def trunk(x, s_gla, s_hg, tm_ffn, tm_mix, blk):
        bsz, seq, _ = x.shape
        x1 = _ffn(x.reshape(bsz * seq, D_MODEL), *f1, tm=tm_ffn, final_norm=False)
        x2, st = _mixer(x1.reshape(bsz, seq, D_MODEL), _pack_state(s_gla, s_hg), *mix_w, tm=tm_mix, blk=blk)
        y = _ffn(x2.reshape(bsz * seq, D_MODEL), *f2, tm=tm_ffn, final_norm=True)
        g, h = _unpack_state(st)
        return y.reshape(bsz, seq, D_MODEL), g[None], h[None]

    bp, sp, _ = x_prompt.shape
    bs, ss, _ = x_sample.shape
    zg = jnp.zeros((bp, GLA_HEADS, GLA_DK, GLA_DV), _F32)
    zh = jnp.zeros((bp, HG_HEADS, HG_EXPAND, HG_HEAD_DIM), _F32)
    y_p, g_p, h_p = trunk(x_prompt, zg, zh, 512, 256, min(CHUNK, sp))
    y_s, g_s, h_s = trunk(x_sample, state_gla[0], state_hgrn[0], bs * ss, ss, ss)
    sd_g, sd_h = state_gla.dtype, state_hgrn.dtype
    return (y_p, y_s, g_p.astype(sd_g), h_p.astype(sd_h), g_s.astype(sd_g), h_s.astype(sd_h))
```

```python
import functools

import jax
import jax.numpy as jnp
from jax import lax
from jax.experimental import pallas as pl
from jax.experimental.pallas import tpu as pltpu

D_MODEL = 1024
CHUNK = 64
GLA_HEADS = 4
GLA_DK = 64
GLA_DV = 128
GLA_GATE_RANK = 16
GLA_GATE_NORM = 16.0
HG_HEADS = 4
HG_EXPAND = 128
HG_HEAD_DIM = 128
D_FF = 2816
EPS = 1e-6

LANES = 128
N_HEADS = GLA_HEADS + HG_HEADS
FF_CHUNK = 256
N_FF_CHUNKS = D_FF // FF_CHUNK
FFN_ROWS = 512
MIX_ROWS = 256
VMEM_LIMIT = 56 * 1024 * 1024

_GD, _HF, _GQ, _GK, _HQ, _GV, _HI, _GR, _HGATE = 0, 128, 640, 896, 1152, 1664, 2176, 2688, 3200
D_PROJ = _HGATE + HG_HEADS * HG_HEAD_DIM
PROJ_GROUPS = (_GD, _GQ, _HQ, _GV, _HI, _GR, _HGATE, D_PROJ)

_F32 = jnp.float32
_BF16 = jnp.bfloat16


def _rmsnorm(x, g):
    return x * lax.rsqrt(jnp.mean(x * x, axis=-1, keepdims=True) + EPS) * g


def _sigmoid(x):
    return 1.0 / (1.0 + jnp.exp(-x))


def _dot(a, b):
    return jnp.dot(a, b, preferred_element_type=_F32)


def _dot_nt(a, b):
    return lax.dot_general(a, b, (((1,), (1,)), ((), ())), preferred_element_type=_F32)


def _dot_tn(a, b):
    return lax.dot_general(a, b, (((0,), (0,)), ((), ())), preferred_element_type=_F32)


def _ffn_kernel(x_ref, g_ref, wg_ref, wu_ref, wd_ref, gf_ref, o_ref, acc_ref, *, final_norm):
    x = x_ref[...]
    h = _rmsnorm(x, g_ref[...]).astype(_BF16)
    for c in range(N_FF_CHUNKS):
        gate = _dot(h, wg_ref[c])
        up = _dot(h, wu_ref[c])
        a = (gate * _sigmoid(gate) * up).astype(_BF16)
        d = _dot(a, wd_ref[c])
        if c == 0:
            acc_ref[...] = d
        else:
            acc_ref[...] += d
    y = x + 0.5 * acc_ref[...]
    if final_norm:
        y = _rmsnorm(y, gf_ref[...])
    o_ref[...] = y


def _const_spec(shape):
    zeros = (0,) * len(shape)
    return pl.BlockSpec(shape, lambda *_: zeros, pipeline_mode=pl.Buffered(1))


def _ffn(x2d, g, wg, wu, wd, gf, *, tm, final_norm):
    n = x2d.shape[0]
    assert n % tm == 0
    row_spec = pl.BlockSpec((tm, D_MODEL), lambda i: (i, 0))
    return pl.pallas_call(
        functools.partial(_ffn_kernel, final_norm=final_norm),
        grid=(n // tm,),
        in_specs=[row_spec, _const_spec(g.shape), _const_spec(wg.shape), _const_spec(wu.shape),
                  _const_spec(wd.shape), _const_spec(gf.shape)],
        out_specs=row_spec,
        out_shape=jax.ShapeDtypeStruct((n, D_MODEL), _F32),
        scratch_shapes=[pltpu.VMEM((tm, D_MODEL), _F32)],
        compiler_params=pltpu.CompilerParams(dimension_semantics=("arbitrary",),
                                             vmem_limit_bytes=VMEM_LIMIT),
        name="ffn_final" if final_norm else "ffn",
    )(x2d, g, wg, wu, wd, gf)


def _mixer_kernel(x_ref, st_in_ref, g_ref, win_ref, gup_ref, gbias_ref, lb_ref, gon_ref, hon_ref, wout_ref,
                  o_ref, st_ref, proj_ref, mix_ref, *, tm, blk):
    t = pl.program_id(1)
    nb = tm // blk

    @pl.when(t == 0)
    def _():
        for h in range(N_HEADS):
            st_ref[h] = st_in_ref[h].T

    x = x_ref[...]
    h_in = _rmsnorm(x, g_ref[...]).astype(_BF16)
    for c0, c1 in zip(PROJ_GROUPS[:-1], PROJ_GROUPS[1:]):
        proj_ref[:, c0:c1] = _dot(h_in, win_ref[:, c0:c1])

    row = lax.broadcasted_iota(jnp.int32, (tm, tm), 0)
    col = lax.broadcasted_iota(jnp.int32, (tm, tm), 1)
    causal = (col <= row) & (col >= row - row % blk)
    tri = causal.astype(_BF16)
    lane = lax.broadcasted_iota(jnp.int32, (tm, LANES), 1)
    lb = lb_ref[...]

    def cumsum_rows(g):
        hi = g.astype(_BF16)
        r1 = g - hi.astype(_F32)
        mid = r1.astype(_BF16)
        lo = (r1 - mid.astype(_F32)).astype(_BF16)
        return _dot(tri, hi) + _dot(tri, mid) + _dot(tri, lo)

    gd = proj_ref[:, _GD:_GD + LANES].astype(_BF16)
    logits = _dot(gd, gup_ref[...]) + gbias_ref[...]
    g_cols = [(jnp.minimum(logits, 0.0) - jnp.log1p(jnp.exp(-jnp.abs(logits)))) / GLA_GATE_NORM]
    k_in = []
    for hd in range(HG_HEADS):
        z = proj_ref[:, _HF + hd * LANES:_HF + (hd + 1) * LANES]
        lb_h = lb[:, hd * LANES:(hd + 1) * LANES]
        e = jnp.exp(-jnp.abs(z))
        r = 1.0 / (1.0 + e)
        sig_pos = jnp.where(z >= 0, r, e * r)
        sig_neg = jnp.where(z >= 0, e * r, r)
        g_cols.append(jnp.log(lb_h + (1.0 - lb_h) * sig_pos))
        k_in.append((1.0 - lb_h) * sig_neg)
    b_all = cumsum_rows(jnp.concatenate(g_cols, axis=1))

    qts, kts, vs, ebs = [], [], [], []
    for p in range(GLA_HEADS // 2):
        b = b_all[:, p * LANES:(p + 1) * LANES]
        eb = jnp.exp(b)
        q = proj_ref[:, _GQ + p * LANES:_GQ + (p + 1) * LANES] * (GLA_DK ** -0.5) * eb
        k = proj_ref[:, _GK + p * LANES:_GK + (p + 1) * LANES] * jnp.exp(-b)
        for hh in range(2):
            idx = 2 * p + hh
            m = (lane >= hh * GLA_DK) & (lane < (hh + 1) * GLA_DK)
            qts.append(jnp.where(m, q, 0.0).astype(_BF16))
            kts.append(jnp.where(m, k, 0.0).astype(_BF16))
            vs.append(proj_ref[:, _GV + idx * GLA_DV:_GV + (idx + 1) * GLA_DV].astype(_BF16))
            ebs.append(eb)
    for hd in range(HG_HEADS):
        b = b_all[:, (2 + hd) * LANES:(3 + hd) * LANES]
        eb = jnp.exp(b)
        hq = proj_ref[:, _HQ + hd * LANES:_HQ + (hd + 1) * LANES]
        qts.append((hq * _sigmoid(hq) * eb).astype(_BF16))
        kts.append((k_in[hd] * jnp.exp(-b)).astype(_BF16))
        vs.append(proj_ref[:, _HI + hd * LANES:_HI + (hd + 1) * LANES].astype(_BF16))
        ebs.append(eb)

    def cat(x, y, axis):
        return jnp.concatenate([x, y], axis=axis)

    def cat_rows(parts):
        return parts[0] if len(parts) == 1 else jnp.concatenate(parts, axis=0)

    def bdiag(x, y):
        return cat(cat(x, jnp.zeros_like(y), 1), cat(jnp.zeros_like(x), y, 1), 0)

    gs = min(tm, LANES)
    groups = [slice(u * gs, (u + 1) * gs) for u in range(tm // gs)]
    blocks = [slice(c * blk, (c + 1) * blk) for c in range(nb)]
    pairs = [(2 * p, 2 * p + 1) for p in range(N_HEADS // 2)]
    row2 = lax.broadcasted_iota(jnp.int32, (gs, 2 * gs), 0)
    col2 = lax.broadcasted_iota(jnp.int32, (gs, 2 * gs), 1)
    col2 = jnp.where(col2 >= gs, col2 - gs, col2)
    causal2 = (col2 <= row2) & (col2 >= row2 - row2 % blk)

    q2 = [cat(qts[a], qts[b], 1) for a, b in pairs]
    scores = [[_dot_nt(q2[p][g], bdiag(kts[a][g], kts[b][g])) for g in groups]
              for p, (a, b) in enumerate(pairs)]
    kvs = [[_dot_tn(cat(vs[a][rs], vs[b][rs], 0), bdiag(kts[a][rs], kts[b][rs])) for rs in blocks]
           for a, b in pairs]
    o_intra = [[_dot(jnp.where(causal2, scores[p][u], 0.0).astype(_BF16), bdiag(vs[a][g], vs[b][g]))
                for u, g in enumerate(groups)] for p, (a, b) in enumerate(pairs)]
    states = [cat(st_ref[a], st_ref[b], 1) for a, b in pairs]
    e2 = [cat(ebs[a], ebs[b], 1) for a, b in pairs]
    o_inter = [[] for _ in pairs]
    for c, rs in enumerate(blocks):
        for p in range(len(pairs)):
            s2 = states[p].astype(_BF16)
            o_inter[p].append(_dot_nt(q2[p][rs], bdiag(s2[:, :LANES], s2[:, LANES:])))
        for p in range(len(pairs)):
            states[p] = (states[p] + kvs[p][c]) * e2[p][(c + 1) * blk - 1:(c + 1) * blk, :]
    for p, (a, b) in enumerate(pairs):
        st_ref[a] = states[p][:, :LANES]
        st_ref[b] = states[p][:, LANES:]
        o2 = cat_rows(o_intra[p]) + cat_rows(o_inter[p])
        for hh, h in enumerate((a, b)):
            o = o2[:, hh * LANES:(hh + 1) * LANES]
            onorm = gon_ref[...] if h < GLA_HEADS else hon_ref[...]
            gate_col = _GR + h * LANES if h < GLA_HEADS else _HGATE + (h - GLA_HEADS) * LANES
            gate = proj_ref[:, gate_col:gate_col + LANES]
            o = o * lax.rsqrt(jnp.mean(o * o, axis=-1, keepdims=True) + EPS) * onorm
            o = o * (gate * _sigmoid(gate))
            mix_ref[:, h * LANES:(h + 1) * LANES] = o.astype(_BF16)

    o_ref[...] = x + _dot(mix_ref[...], wout_ref[...])

    @pl.when(t == pl.num_programs(1) - 1)
    def _():
        for h in range(N_HEADS):
            st_ref[h] = st_ref[h].T


def _mixer(x, st_in, g, win, gup, gbias, lb, gon, hon, wout, *, tm, blk):
    bsz, seq, _ = x.shape
    assert seq % tm == 0 and tm % blk == 0
    row_spec = pl.BlockSpec((None, tm, D_MODEL), lambda b, t: (b, t, 0))
    st_spec = pl.BlockSpec((None, N_HEADS, LANES, LANES), lambda b, t: (b, 0, 0, 0))
    return pl.pallas_call(
        functools.partial(_mixer_kernel, tm=tm, blk=blk),
        grid=(bsz, seq // tm),
        in_specs=[row_spec, st_spec] + [_const_spec(a.shape) for a in (g, win, gup, gbias, lb, gon, hon, wout)],
        out_specs=[row_spec, st_spec],
        out_shape=[jax.ShapeDtypeStruct(x.shape, _F32),
                   jax.ShapeDtypeStruct((bsz, N_HEADS, LANES, LANES), _F32)],
        scratch_shapes=[pltpu.VMEM((tm, D_PROJ), _F32), pltpu.VMEM((tm, D_MODEL), _BF16)],
        compiler_params=pltpu.CompilerParams(dimension_semantics=("arbitrary", "arbitrary"),
                                             vmem_limit_bytes=VMEM_LIMIT),
        name="mixer",
    )(x, st_in, g, win, gup, gbias, lb, gon, hon, wout)


def _pack_state(s_gla, s_hg):
    zeros = jnp.zeros_like(s_gla)
    even = jnp.concatenate([s_gla, zeros], axis=2)
    odd = jnp.concatenate([zeros, s_gla], axis=2)
    is_odd = (jnp.arange(GLA_HEADS) % 2 == 1)[None, :, None, None]
    gla = jnp.where(is_odd, odd, even)
    return jnp.concatenate([gla, s_hg], axis=1).astype(_F32)


def _unpack_state(st):
    gla = st[:, :GLA_HEADS]
    is_odd = (jnp.arange(GLA_HEADS) % 2 == 1)[None, :, None, None]
    gla = jnp.where(is_odd, gla[:, :, GLA_DK:], gla[:, :, :GLA_DK])
    return gla, st[:, GLA_HEADS:]


def _ff_chunks_cols(w):
    return w.reshape(w.shape[0], N_FF_CHUNKS, FF_CHUNK).transpose(1, 0, 2).astype(_BF16)


def _ff_chunks_rows(w):
    return w.reshape(N_FF_CHUNKS, FF_CHUNK, w.shape[1]).astype(_BF16)


def kernel(x_prompt, x_sample, state_gla, state_hgrn, norm_ffn1, ffn1_w_gate, ffn1_w_up, ffn1_w_down, norm_mix, w_in, gla_gate_up, gla_gate_bias, gla_onorm, hg_lower_bound_logits, hg_onorm, w_out, norm_ffn2, ffn2_w_gate, ffn2_w_up, ffn2_w_down, norm_final):
    assert norm_ffn1.shape[0] == 1, "single layer"
    row = lambda v: v.reshape(1, -1).astype(_F32)
    f1 = (row(norm_ffn1[0]), _ff_chunks_cols(ffn1_w_gate[0]), _ff_chunks_cols(ffn1_w_up[0]),
          _ff_chunks_rows(ffn1_w_down[0]), row(norm_final))
    f2 = (row(norm_ffn2[0]), _ff_chunks_cols(ffn2_w_gate[0]), _ff_chunks_cols(ffn2_w_up[0]),
          _ff_chunks_rows(ffn2_w_down[0]), row(norm_final))
    sizes = (GLA_HEADS * GLA_DK, GLA_HEADS * GLA_DK, GLA_HEADS * GLA_DV, GLA_HEADS * GLA_DV, GLA_GATE_RANK,
             HG_HEADS * HG_EXPAND, HG_HEADS * HG_EXPAND, HG_HEADS * HG_HEAD_DIM, HG_HEADS * HG_HEAD_DIM)
    splits = [sum(sizes[:i + 1]) for i in range(len(sizes) - 1)]
    gq, gk, gv, gr, gd, hq, hf, hi, hgate = jnp.split(w_in[0], splits, axis=1)
    gd = jnp.concatenate([gd, jnp.zeros((D_MODEL, LANES - GLA_GATE_RANK), gd.dtype)], axis=1)
    win = jnp.concatenate([gd, hf, gq, gk, hq, gv, hi, gr, hgate], axis=1).astype(_BF16)
    gup = jnp.concatenate([gla_gate_up[0], jnp.zeros((LANES - GLA_GATE_RANK, GLA_HEADS * GLA_DK), _F32)],
                          axis=0).astype(_BF16)
    lb = jnp.cumsum(jax.nn.softmax(hg_lower_bound_logits.astype(_F32), axis=0), axis=0)[0]
    mix_w = (row(norm_mix[0]), win, gup, row(gla_gate_bias[0]), row(lb), row(gla_onorm[0]), row(hg_onorm[0]),
             w_out[0].astype(_BF16))

    def trunk(x, s_gla, s_hg, tm_ffn, tm_mix, blk):
        bsz, seq, _ = x.shape
        x1 = _ffn(x.reshape(bsz * seq, D_MODEL), *f1, tm=tm_ffn, final_norm=False)
        x2, st = _mixer(x1.reshape(bsz, seq, D_MODEL), _pack_state(s_gla, s_hg), *mix_w, tm=tm_mix, blk=blk)
        y = _ffn(x2.reshape(bsz * seq, D_MODEL), *f2, tm=tm_ffn, final_norm=True)
        g, h = _unpack_state(st)
        return y.reshape(bsz, seq, D_MODEL), g[None], h[None]

    bp, sp, _ = x_prompt.shape
    bs, ss, _ = x_sample.shape
    zg = jnp.zeros((bp, GLA_HEADS, GLA_DK, GLA_DV), _F32)
    zh = jnp.zeros((bp, HG_HEADS, HG_EXPAND, HG_HEAD_DIM), _F32)
    blk_p = min(CHUNK, sp)
    y_p, g_p, h_p = trunk(x_prompt, zg, zh, FFN_ROWS, min(MIX_ROWS, sp), blk_p)
    y_s, g_s, h_s = trunk(x_sample, state_gla[0], state_hgrn[0], bs * ss, ss, ss)
    sd_g, sd_h = state_gla.dtype, state_hgrn.dtype
    return (y_p, y_s, g_p.astype(sd_g), h_p.astype(sd_h), g_s.astype(sd_g), h_s.astype(sd_h))
```

```python
import functools

import jax
import jax.numpy as jnp
from jax import lax
from jax.experimental import pallas as pl
from jax.experimental.pallas import tpu as pltpu

D_MODEL = 1024
CHUNK = 64
GLA_HEADS = 4
GLA_DK = 64
GLA_DV = 128
GLA_GATE_RANK = 16
GLA_GATE_NORM = 16.0
HG_HEADS = 4
HG_EXPAND = 128
HG_HEAD_DIM = 128
D_FF = 2816
EPS = 1e-6

LANES = 128
MXU_COLS = 256
N_HEADS = GLA_HEADS + HG_HEADS
FF_CHUNK = MXU_COLS
N_FF_CHUNKS = D_FF // FF_CHUNK
FFN_ROWS = 512
MIX_ROWS = 256
VMEM_LIMIT = 56 * 1024 * 1024

_GD, _HF, _GQ, _GK, _HQ, _GV, _HI, _GR, _HGATE = 0, 256, 768, 1024, 1280, 1792, 2304, 2816, 3328
D_PROJ = _HGATE + HG_HEADS * HG_HEAD_DIM
PROJ_PIECE = 2 * MXU_COLS
N_PROJ_PIECES = -(-D_PROJ // PROJ_PIECE)
PROJ_SCHEDULE_A = (2, 2, 1, 1, 0, 1, 1)
PROJ_SCHEDULE_B = (2, 2, 1, 1, 0, 1, 1)
assert sum(PROJ_SCHEDULE_A) == sum(PROJ_SCHEDULE_B) == N_PROJ_PIECES
TILE_ORDER = "aaaaaabbbbbb"

_F32 = jnp.float32
_BF16 = jnp.bfloat16


def _rmsnorm(x, g):
    return x * lax.rsqrt(jnp.mean(x * x, axis=-1, keepdims=True) + EPS) * g


def _sigmoid(x):
    return 1.0 / (1.0 + jnp.exp(-x))


def _dot(a, b):
    return jnp.dot(a, b, preferred_element_type=_F32)


_NT = (((1,), (1,)), ((), ()))
_TN = (((0,), (0,)), ((), ()))


def _ffn_kernel(x_ref, g_ref, wg_ref, wu_ref, wd_ref, gf_ref, o_ref, acc_ref, *, final_norm):
    x = x_ref[...]
    h = _rmsnorm(x, g_ref[...]).astype(_BF16)
    for c in range(N_FF_CHUNKS):
        cols = slice(c * FF_CHUNK, (c + 1) * FF_CHUNK)
        gate = _dot(h, wg_ref[:, cols])
        up = _dot(h, wu_ref[:, cols])
        a = (gate * _sigmoid(gate) * up).astype(_BF16)
        d = _dot(a, wd_ref[cols, :])
        if c == 0:
            acc_ref[...] = d
        else:
            acc_ref[...] += d
    y = x + 0.5 * acc_ref[...]
    if final_norm:
        y = _rmsnorm(y, gf_ref[...])
    o_ref[...] = y


def _const_spec(shape):
    zeros = (0,) * len(shape)
    return pl.BlockSpec(shape, lambda *_: zeros, pipeline_mode=pl.Buffered(1))


def _ffn(x2d, g, wg, wu, wd, gf, *, tm, final_norm):
    n = x2d.shape[0]
    assert n % tm == 0
    row_spec = pl.BlockSpec((tm, D_MODEL), lambda i: (i, 0))
    return pl.pallas_call(
        functools.partial(_ffn_kernel, final_norm=final_norm),
        grid=(n // tm,),
        in_specs=[row_spec, _const_spec(g.shape), _const_spec(wg.shape), _const_spec(wu.shape),
                  _const_spec(wd.shape), _const_spec(gf.shape)],
        out_specs=row_spec,
        out_shape=jax.ShapeDtypeStruct((n, D_MODEL), _F32),
        scratch_shapes=[pltpu.VMEM((tm, D_MODEL), _F32)],
        compiler_params=pltpu.CompilerParams(dimension_semantics=("arbitrary",),
                                             vmem_limit_bytes=VMEM_LIMIT),
        name="ffn_final" if final_norm else "ffn",
    )(x2d, g, wg, wu, wd, gf)


def _mixer_tile(pcur, x_ref, o_ref, mix_ref, st_ref, gup_ref, gbias_ref, lb_ref, gon_ref, hon_ref, wout_ref,
                *, r0, tm, blk, fill):
    nb = tm // blk
    rows = slice(r0, r0 + tm)
    row = lax.broadcasted_iota(jnp.int32, (tm, tm), 0)
    col = lax.broadcasted_iota(jnp.int32, (tm, tm), 1)
    tri = ((col <= row) & (col >= row - row % blk)).astype(_BF16)
    lane = lax.broadcasted_iota(jnp.int32, (tm, LANES), 1)
    lb = lb_ref[...]

    def cumsum_rows(g):
        hi = g.astype(_BF16)
        r1 = g - hi.astype(_F32)
        mid = r1.astype(_BF16)
        lo = (r1 - mid.astype(_F32)).astype(_BF16)
        n = g.shape[1]
        parts = jnp.concatenate([hi, mid, lo], axis=1)
        r = jnp.dot(tri, parts, preferred_element_type=_F32)
        return r[:, :n] + r[:, n:2 * n] + r[:, 2 * n:]

    def cat(x, y, axis):
        return jnp.concatenate([x, y], axis=axis)

    def cat_rows(parts):
        return parts[0] if len(parts) == 1 else jnp.concatenate(parts, axis=0)

    def bdiag(x, y):
        return cat(cat(x, jnp.zeros_like(y), 1), cat(jnp.zeros_like(x), y, 1), 0)

    gd = pcur(_GD, _GD + LANES).astype(_BF16)
    logits = jnp.dot(gd, gup_ref[...], preferred_element_type=_F32) + gbias_ref[...]
    g_cols = [(jnp.minimum(logits, 0.0) - jnp.log1p(jnp.exp(-jnp.abs(logits)))) / GLA_GATE_NORM]
    k_in = []
    for hd in range(HG_HEADS):
        z = pcur(_HF + hd * LANES, _HF + (hd + 1) * LANES)
        lb_h = lb[:, hd * LANES:(hd + 1) * LANES]
        e = jnp.exp(-jnp.abs(z))
        r = 1.0 / (1.0 + e)
        sig_pos = jnp.where(z >= 0, r, e * r)
        sig_neg = jnp.where(z >= 0, e * r, r)
        g_cols.append(jnp.log(lb_h + (1.0 - lb_h) * sig_pos))
        k_in.append((1.0 - lb_h) * sig_neg)
    fill(0)
    b_all = cumsum_rows(jnp.concatenate(g_cols, axis=1))
    yield

    qts, kts, vs, ebs = [], [], [], []
    for p in range(GLA_HEADS // 2):
        b = b_all[:, p * LANES:(p + 1) * LANES]
        eb = jnp.exp(b)
        q = pcur(_GQ + p * LANES, _GQ + (p + 1) * LANES) * (GLA_DK ** -0.5) * eb
        k = pcur(_GK + p * LANES, _GK + (p + 1) * LANES) * jnp.exp(-b)
        for hh in range(2):
            idx = 2 * p + hh
            m = (lane >= hh * GLA_DK) & (lane < (hh + 1) * GLA_DK)
            qts.append(jnp.where(m, q, 0.0).astype(_BF16))
            kts.append(jnp.where(m, k, 0.0).astype(_BF16))
            vs.append(pcur(_GV + idx * GLA_DV, _GV + (idx + 1) * GLA_DV).astype(_BF16))
            ebs.append(eb)
    for hd in range(HG_HEADS):
        b = b_all[:, (2 + hd) * LANES:(3 + hd) * LANES]
        eb = jnp.exp(b)
        hq = pcur(_HQ + hd * LANES, _HQ + (hd + 1) * LANES)
        qts.append((hq * _sigmoid(hq) * eb).astype(_BF16))
        kts.append((k_in[hd] * jnp.exp(-b)).astype(_BF16))
        vs.append(pcur(_HI + hd * LANES, _HI + (hd + 1) * LANES).astype(_BF16))
        ebs.append(eb)

    gs = min(tm, LANES)
    groups = [slice(u * gs, (u + 1) * gs) for u in range(tm // gs)]
    blocks = [slice(c * blk, (c + 1) * blk) for c in range(nb)]
    pairs = [(2 * p, 2 * p + 1) for p in range(N_HEADS // 2)]
    row2 = lax.broadcasted_iota(jnp.int32, (gs, 2 * gs), 0)
    col2 = lax.broadcasted_iota(jnp.int32, (gs, 2 * gs), 1)
    col2 = jnp.where(col2 >= gs, col2 - gs, col2)
    causal2 = (col2 <= row2) & (col2 >= row2 - row2 % blk)

    q2 = [cat(qts[a], qts[b], 1) for a, b in pairs]
    fill(1)
    scores = [[lax.dot_general(q2[p][g], bdiag(kts[a][g], kts[b][g]), _NT, preferred_element_type=_F32)
               for g in groups] for p, (a, b) in enumerate(pairs)]
    kvs = [[lax.dot_general(cat(vs[a][rs], vs[b][rs], 0), bdiag(kts[a][rs], kts[b][rs]), _TN,
                            preferred_element_type=_F32)
            for rs in blocks] for a, b in pairs]
    yield
    fill(2)
    o_intra = [[jnp.dot(jnp.where(causal2, scores[p][u], 0.0).astype(_BF16), bdiag(vs[a][g], vs[b][g]),
                        preferred_element_type=_F32)
                for u, g in enumerate(groups)] for p, (a, b) in enumerate(pairs)]
    yield
    fill(3)
    states = [cat(st_ref[a], st_ref[b], 1) for a, b in pairs]
    e2 = [cat(ebs[a], ebs[b], 1) for a, b in pairs]
    o_inter = [[] for _ in pairs]
    for c, rs in enumerate(blocks):
        for p in range(len(pairs)):
            s2 = states[p].astype(_BF16)
            o_inter[p].append(lax.dot_general(q2[p][rs], bdiag(s2[:, :LANES], s2[:, LANES:]), _NT,
                                              preferred_element_type=_F32))
        for p in range(len(pairs)):
            states[p] = (states[p] + kvs[p][c]) * e2[p][(c + 1) * blk - 1:(c + 1) * blk, :]
    yield
    fill(4)
    for p, (a, b) in enumerate(pairs):
        st_ref[a] = states[p][:, :LANES]
        st_ref[b] = states[p][:, LANES:]
        o2 = cat_rows(o_intra[p]) + cat_rows(o_inter[p])
        for hh, h in enumerate((a, b)):
            o = o2[:, hh * LANES:(hh + 1) * LANES]
            onorm = gon_ref[...] if h < GLA_HEADS else hon_ref[...]
            gate_col = _GR + h * LANES if h < GLA_HEADS else _HGATE + (h - GLA_HEADS) * LANES
            gate = pcur(gate_col, gate_col + LANES)
            o = o * lax.rsqrt(jnp.mean(o * o, axis=-1, keepdims=True) + EPS) * onorm
            o = o * (gate * _sigmoid(gate))
            mix_ref[rows, h * LANES:(h + 1) * LANES] = o.astype(_BF16)
    yield
    fill(5)
    out = jnp.dot(mix_ref[rows, :], wout_ref[...], preferred_element_type=_F32)
    fill(6)
    o_ref[rows, :] = x_ref[rows, :] + out


def _mixer_kernel(x_ref, xn_ref, st_in_ref, g_ref, win_ref, gup_ref, gbias_ref, lb_ref, gon_ref, hon_ref, wout_ref,
                  o_ref, st_ref, pa_ref, pb_ref, ha_ref, hb_ref, mix_ref, *, tm, blk, pipelined):
    u = pl.program_id(1)
    tile = functools.partial(_mixer_tile, x_ref=x_ref, o_ref=o_ref, mix_ref=mix_ref, st_ref=st_ref,
                             gup_ref=gup_ref, gbias_ref=gbias_ref, lb_ref=lb_ref, gon_ref=gon_ref,
                             hon_ref=hon_ref, wout_ref=wout_ref, tm=tm, blk=blk)

    def normed(src_ref, r0):
        return _rmsnorm(src_ref[r0:r0 + tm, :], g_ref[...]).astype(_BF16)

    valid = {id(pa_ref): set(range(N_PROJ_PIECES)), id(pb_ref): set()}

    def project(h, dst_ref, pieces, for_reader):
        for i in pieces:
            cols = slice(i * PROJ_PIECE, min((i + 1) * PROJ_PIECE, D_PROJ))
            dst_ref[:, cols] = jnp.dot(h, win_ref[:, cols], preferred_element_type=_F32)
            (valid[id(dst_ref)].add if for_reader else valid[id(dst_ref)].discard)(i)

    def reader(buf_ref):
        def read(c0, c1):
            assert all(i in valid[id(buf_ref)] for i in range(c0 // PROJ_PIECE, (c1 - 1) // PROJ_PIECE + 1))
            return buf_ref[:, c0:c1]
        return read

    def filler(h_ref, dst_ref, schedule, for_reader):
        pieces = iter(range(N_PROJ_PIECES))
        return lambda stage: project(h_ref[...], dst_ref, [next(pieces) for _ in range(schedule[stage])],
                                     for_reader)

    @pl.when(u == 0)
    def _():
        for h in range(N_HEADS):
            st_ref[h] = st_in_ref[h].T

    if pipelined:
        @pl.when((pl.program_id(0) == 0) & (u == 0))
        def _():
            project(normed(x_ref, 0), pa_ref, range(N_PROJ_PIECES), True)
            hb_ref[...] = normed(x_ref, tm)

        ha_ref[...] = normed(xn_ref, 0)
        stages_a = tile(reader(pa_ref), r0=0, fill=filler(hb_ref, pb_ref, PROJ_SCHEDULE_A, True))
        stages_b = tile(reader(pb_ref), r0=tm, fill=filler(ha_ref, pa_ref, PROJ_SCHEDULE_B, False))
        for turn in TILE_ORDER:
            next(stages_a if turn == "a" else stages_b, None)
        hb_ref[...] = normed(xn_ref, tm)
    else:
        project(normed(x_ref, 0), pa_ref, range(N_PROJ_PIECES), True)
        for _ in tile(reader(pa_ref), r0=0, fill=lambda stage: None):
            pass

    @pl.when(u == pl.num_programs(1) - 1)
    def _():
        for h in range(N_HEADS):
            st_ref[h] = st_ref[h].T


def _mixer(x, st_in, g, win, gup, gbias, lb, gon, hon, wout, *, tm, blk):
    bsz, seq, _ = x.shape
    pipelined = seq % (2 * tm) == 0
    rows = 2 * tm if pipelined else tm
    assert seq % rows == 0 and tm % blk == 0
    nu = seq // rows
    last = bsz * nu - 1

    def next_block(b, u):
        nxt = jnp.minimum(b * nu + u + 1, last)
        return (nxt // nu, nxt % nu, 0)

    row_spec = pl.BlockSpec((None, rows, D_MODEL), lambda b, u: (b, u, 0))
    st_spec = pl.BlockSpec((None, N_HEADS, LANES, LANES), lambda b, u: (b, 0, 0, 0))
    return pl.pallas_call(
        functools.partial(_mixer_kernel, tm=tm, blk=blk, pipelined=pipelined),
        grid=(bsz, nu),
        in_specs=[row_spec, pl.BlockSpec((None, rows, D_MODEL), next_block), st_spec]
        + [_const_spec(a.shape) for a in (g, win, gup, gbias, lb, gon, hon, wout)],
        out_specs=[row_spec, st_spec],
        out_shape=[jax.ShapeDtypeStruct(x.shape, _F32),
                   jax.ShapeDtypeStruct((bsz, N_HEADS, LANES, LANES), _F32)],
        scratch_shapes=[pltpu.VMEM((tm, D_PROJ), _F32), pltpu.VMEM((tm, D_PROJ), _F32),
                        pltpu.VMEM((tm, D_MODEL), _BF16), pltpu.VMEM((tm, D_MODEL), _BF16),
                        pltpu.VMEM((rows, D_MODEL), _BF16)],
        compiler_params=pltpu.CompilerParams(dimension_semantics=("arbitrary", "arbitrary"),
                                             vmem_limit_bytes=VMEM_LIMIT),
        name="mixer",
    )(x, x, st_in, g, win, gup, gbias, lb, gon, hon, wout)


def _pack_state(s_gla, s_hg):
    zeros = jnp.zeros_like(s_gla)
    even = jnp.concatenate([s_gla, zeros], axis=2)
    odd = jnp.concatenate([zeros, s_gla], axis=2)
    is_odd = (jnp.arange(GLA_HEADS) % 2 == 1)[None, :, None, None]
    gla = jnp.where(is_odd, odd, even)
    return jnp.concatenate([gla, s_hg], axis=1).astype(_F32)


def _unpack_state(st):
    gla = st[:, :GLA_HEADS]
    is_odd = (jnp.arange(GLA_HEADS) % 2 == 1)[None, :, None, None]
    gla = jnp.where(is_odd, gla[:, :, GLA_DK:], gla[:, :, :GLA_DK])
    return gla, st[:, GLA_HEADS:]


def kernel(x_prompt, x_sample, state_gla, state_hgrn, norm_ffn1, ffn1_w_gate, ffn1_w_up, ffn1_w_down, norm_mix, w_in, gla_gate_up, gla_gate_bias, gla_onorm, hg_lower_bound_logits, hg_onorm, w_out, norm_ffn2, ffn2_w_gate, ffn2_w_up, ffn2_w_down, norm_final):
    assert norm_ffn1.shape[0] == 1, "single layer"
    row = lambda v: v.reshape(1, -1).astype(_F32)
    bf = lambda w: w.astype(_BF16)
    f1 = (row(norm_ffn1[0]), bf(ffn1_w_gate[0]), bf(ffn1_w_up[0]), bf(ffn1_w_down[0]), row(norm_final))
    f2 = (row(norm_ffn2[0]), bf(ffn2_w_gate[0]), bf(ffn2_w_up[0]), bf(ffn2_w_down[0]), row(norm_final))
    sizes = (GLA_HEADS * GLA_DK, GLA_HEADS * GLA_DK, GLA_HEADS * GLA_DV, GLA_HEADS * GLA_DV, GLA_GATE_RANK,
             HG_HEADS * HG_EXPAND, HG_HEADS * HG_EXPAND, HG_HEADS * HG_HEAD_DIM, HG_HEADS * HG_HEAD_DIM)
    splits = [sum(sizes[:i + 1]) for i in range(len(sizes) - 1)]
    gq, gk, gv, gr, gd, hq, hf, hi, hgate = jnp.split(w_in[0], splits, axis=1)
    gd = jnp.concatenate([gd, jnp.zeros((D_MODEL, _HF - GLA_GATE_RANK), gd.dtype)], axis=1)
    win = jnp.concatenate([gd, hf, gq, gk, hq, gv, hi, gr, hgate], axis=1).astype(_BF16)
    gup = jnp.concatenate([gla_gate_up[0], jnp.zeros((LANES - GLA_GATE_RANK, GLA_HEADS * GLA_DK), _F32)],
                          axis=0).astype(_BF16)
    lb = jnp.cumsum(jax.nn.softmax(hg_lower_bound_logits.astype(_F32), axis=0), axis=0)[0]
    mix_w = (row(norm_mix[0]), win, gup, row(gla_gate_bias[0]), row(lb), row(gla_onorm[0]), row(hg_onorm[0]),
             bf(w_out[0]))

    def trunk(x, s_gla, s_hg, tm_ffn, tm_mix, blk):
        bsz, seq, _ = x.shape
        x1 = _ffn(x.reshape(bsz * seq, D_MODEL), *f1, tm=tm_ffn, final_norm=False)
        x2, st = _mixer(x1.reshape(bsz, seq, D_MODEL), _pack_state(s_gla, s_hg), *mix_w, tm=tm_mix, blk=blk)
        y = _ffn(x2.reshape(bsz * seq, D_MODEL), *f2, tm=tm_ffn, final_norm=True)
        g, h = _unpack_state(st)
        return y.reshape(bsz, seq, D_MODEL), g[None], h[None]

    bp, sp, _ = x_prompt.shape
    bs, ss, _ = x_sample.shape
    zg = jnp.zeros((bp, GLA_HEADS, GLA_DK, GLA_DV), _F32)
    zh = jnp.zeros((bp, HG_HEADS, HG_EXPAND, HG_HEAD_DIM), _F32)
    blk_p = min(CHUNK, sp)
    y_p, g_p, h_p = trunk(x_prompt, zg, zh, FFN_ROWS, min(MIX_ROWS, sp), blk_p)
    y_s, g_s, h_s = trunk(x_sample, state_gla[0], state_hgrn[0], bs * ss, ss, ss)
    sd_g, sd_h = state_gla.dtype, state_hgrn.dtype
    return (y_p, y_s, g_p.astype(sd_g), h_p.astype(sd_h), g_s.astype(sd_g), h_s.astype(sd_h))
```

```python
import functools

import jax
import jax.numpy as jnp
from jax import lax
from jax.experimental import pallas as pl
from jax.experimental.pallas import tpu as pltpu

D_MODEL = 1024
CHUNK = 64
GLA_HEADS = 4
GLA_DK = 64
GLA_DV = 128
GLA_GATE_RANK = 16
GLA_GATE_NORM = 16.0
HG_HEADS = 4
HG_EXPAND = 128
HG_HEAD_DIM = 128
D_FF = 2816
EPS = 1e-6

LANES = 128
MXU_COLS = 256
N_HEADS = GLA_HEADS + HG_HEADS
FF_CHUNK = MXU_COLS
N_FF_CHUNKS = D_FF // FF_CHUNK
FFN_ROWS = 1024
FFN_SUB_ROWS = 512
FFN_LEAD = 6
FFN_NORM_PIECES = 4
MIX_ROWS = 256
VMEM_LIMIT = 56 * 1024 * 1024

_GD, _HF, _GQ, _GK, _HQ, _GV, _HI, _GR, _HGATE = 0, 256, 768, 1024, 1280, 1792, 2304, 2816, 3328
D_PROJ = _HGATE + HG_HEADS * HG_HEAD_DIM
PROJ_PIECE = 2 * MXU_COLS
N_PROJ_PIECES = -(-D_PROJ // PROJ_PIECE)
PROJ_SCHEDULE_A = (2, 2, 1, 1, 0, 1, 1)
PROJ_SCHEDULE_B = (2, 2, 1, 1, 0, 1, 1)
assert sum(PROJ_SCHEDULE_A) == sum(PROJ_SCHEDULE_B) == N_PROJ_PIECES
TILE_ORDER = "aaaaaabbbbbb"

_F32 = jnp.float32
_BF16 = jnp.bfloat16


def _rmsnorm(x, g):
    return x * lax.rsqrt(jnp.mean(x * x, axis=-1, keepdims=True) + EPS) * g


def _sigmoid(x):
    return 1.0 / (1.0 + jnp.exp(-x))


def _dot(a, b):
    return jnp.dot(a, b, preferred_element_type=_F32)


_NT = (((1,), (1,)), ((), ()))
_TN = (((0,), (0,)), ((), ()))


def _ffn_subtile(rows, h_ref, x_ref, wg_ref, wu_ref, wd_ref, gf_ref, o_ref, acc_ref, final_norm, anchor=None):
    for c in range(N_FF_CHUNKS):
        cols = slice(c * FF_CHUNK, (c + 1) * FF_CHUNK)
        gate = _dot(h_ref[...], wg_ref[:, cols])
        up = _dot(h_ref[...], wu_ref[:, cols])
        if anchor is not None and anchor[0] == c:
            up = up + anchor[1]()
        a = (gate * _sigmoid(gate) * up).astype(_BF16)
        d = _dot(a, wd_ref[cols, :])
        if c == 0:
            acc_ref[rows, :] = d
        else:
            acc_ref[rows, :] += d
        yield
    y = x_ref[rows, :] + 0.5 * acc_ref[rows, :]
    if final_norm:
        y = _rmsnorm(y, gf_ref[...])
    o_ref[rows, :] = y


def _ffn_kernel(x_ref, xn_ref, g_ref, wg_ref, wu_ref, wd_ref, gf_ref, o_ref, acc_ref, h_ref,
                *, sub_rows, final_norm):
    n_sub = x_ref.shape[0] // sub_rows
    piece = sub_rows // FFN_NORM_PIECES

    def norm_piece(src_ref, r0, dst, k):
        if 0 <= k < FFN_NORM_PIECES:
            src = src_ref[r0 + k * piece:r0 + (k + 1) * piece, :]
            h_ref[dst, k * piece:(k + 1) * piece, :] = _rmsnorm(src, g_ref[...]).astype(_BF16)

    @pl.when(pl.program_id(0) == 0)
    def _():
        for k in range(FFN_NORM_PIECES):
            norm_piece(x_ref, 0, 0, k)

    def zeros_after_next_norm():
        u = pltpu.bitcast(h_ref[0], jnp.uint32)
        z = u[0:8, 0:LANES]
        for r in range(0, u.shape[0], 8):
            for l in range(0, u.shape[1], LANES):
                z = z | u[r:r + 8, l:l + LANES]
        row = pltpu.bitcast((z >> 16) >> 16, _F32)[0:1, :]
        return jnp.concatenate([row] * (FF_CHUNK // LANES), axis=1)

    anchor_chunk = N_FF_CHUNKS + FFN_NORM_PIECES - (n_sub - 1) * FFN_LEAD
    anchor = (anchor_chunk, zeros_after_next_norm) if n_sub > 1 and 0 <= anchor_chunk < N_FF_CHUNKS else None
    subs = [_ffn_subtile(slice(s * sub_rows, (s + 1) * sub_rows), h_ref.at[s], x_ref, wg_ref, wu_ref, wd_ref,
                         gf_ref, o_ref, acc_ref, final_norm, anchor if s == n_sub - 1 else None)
            for s in range(n_sub)]
    n_ticks = max((n_sub - 1) * FFN_LEAD + N_FF_CHUNKS + 1, N_FF_CHUNKS + FFN_NORM_PIECES)
    for tick in range(n_ticks):
        for s, sub in enumerate(subs):
            if tick >= s * FFN_LEAD:
                next(sub, None)
        for s in range(1, n_sub):
            norm_piece(x_ref, s * sub_rows, s, tick - (s - 1) * FFN_LEAD - 1)
        norm_piece(xn_ref, 0, 0, tick - N_FF_CHUNKS)


def _const_spec(shape):
    zeros = (0,) * len(shape)
    return pl.BlockSpec(shape, lambda *_: zeros, pipeline_mode=pl.Buffered(1))


def _ffn(x2d, g, wg, wu, wd, gf, *, tm, final_norm):
    n = x2d.shape[0]
    sub_rows = min(tm, FFN_SUB_ROWS)
    assert n % tm == 0 and tm % sub_rows == 0
    row_spec = pl.BlockSpec((tm, D_MODEL), lambda i: (i, 0))
    n_sub, last_sub = tm // sub_rows, n // sub_rows - 1
    next_spec = pl.BlockSpec((sub_rows, D_MODEL), lambda i: (jnp.minimum((i + 1) * n_sub, last_sub), 0))
    return pl.pallas_call(
        functools.partial(_ffn_kernel, sub_rows=sub_rows, final_norm=final_norm),
        grid=(n // tm,),
        in_specs=[row_spec, next_spec, _const_spec(g.shape), _const_spec(wg.shape), _const_spec(wu.shape),
                  _const_spec(wd.shape), _const_spec(gf.shape)],
        out_specs=row_spec,
        out_shape=jax.ShapeDtypeStruct((n, D_MODEL), _F32),
        scratch_shapes=[pltpu.VMEM((tm, D_MODEL), _F32), pltpu.VMEM((n_sub, sub_rows, D_MODEL), _BF16)],
        compiler_params=pltpu.CompilerParams(dimension_semantics=("arbitrary",),
                                             vmem_limit_bytes=VMEM_LIMIT),
        name="ffn_final" if final_norm else "ffn",
    )(x2d, x2d, g, wg, wu, wd, gf)


def _mixer_tile(pcur, x_ref, o_ref, mix_ref, st_ref, gup_ref, gbias_ref, lb_ref, gon_ref, hon_ref, wout_ref,
                *, r0, tm, blk, fill):
    nb = tm // blk
    rows = slice(r0, r0 + tm)
    row = lax.broadcasted_iota(jnp.int32, (tm, tm), 0)
    col = lax.broadcasted_iota(jnp.int32, (tm, tm), 1)
    tri = ((col <= row) & (col >= row - row % blk)).astype(_BF16)
    lane = lax.broadcasted_iota(jnp.int32, (tm, LANES), 1)
    lb = lb_ref[...]

    def cumsum_rows(g):
        hi = g.astype(_BF16)
        r1 = g - hi.astype(_F32)
        mid = r1.astype(_BF16)
        lo = (r1 - mid.astype(_F32)).astype(_BF16)
        n = g.shape[1]
        parts = jnp.concatenate([hi, mid, lo], axis=1)
        r = jnp.dot(tri, parts, preferred_element_type=_F32)
        return r[:, :n] + r[:, n:2 * n] + r[:, 2 * n:]

    def cat(x, y, axis):
        return jnp.concatenate([x, y], axis=axis)

    def cat_rows(parts):
        return parts[0] if len(parts) == 1 else jnp.concatenate(parts, axis=0)

    def bdiag(x, y):
        return cat(cat(x, jnp.zeros_like(y), 1), cat(jnp.zeros_like(x), y, 1), 0)

    gd = pcur(_GD, _GD + LANES).astype(_BF16)
    logits = jnp.dot(gd, gup_ref[...], preferred_element_type=_F32) + gbias_ref[...]
    g_cols = [(jnp.minimum(logits, 0.0) - jnp.log1p(jnp.exp(-jnp.abs(logits)))) / GLA_GATE_NORM]
    k_in = []
    for hd in range(HG_HEADS):
        z = pcur(_HF + hd * LANES, _HF + (hd + 1) * LANES)
        lb_h = lb[:, hd * LANES:(hd + 1) * LANES]
        e = jnp.exp(-jnp.abs(z))
        r = 1.0 / (1.0 + e)
        sig_pos = jnp.where(z >= 0, r, e * r)
        sig_neg = jnp.where(z >= 0, e * r, r)
        g_cols.append(jnp.log(lb_h + (1.0 - lb_h) * sig_pos))
        k_in.append((1.0 - lb_h) * sig_neg)
    fill(0)
    b_all = cumsum_rows(jnp.concatenate(g_cols, axis=1))
    yield

    qts, kts, vs, ebs = [], [], [], []
    for p in range(GLA_HEADS // 2):
        b = b_all[:, p * LANES:(p + 1) * LANES]
        eb = jnp.exp(b)
        q = pcur(_GQ + p * LANES, _GQ + (p + 1) * LANES) * (GLA_DK ** -0.5) * eb
        k = pcur(_GK + p * LANES, _GK + (p + 1) * LANES) * jnp.exp(-b)
        for hh in range(2):
            idx = 2 * p + hh
            m = (lane >= hh * GLA_DK) & (lane < (hh + 1) * GLA_DK)
            qts.append(jnp.where(m, q, 0.0).astype(_BF16))
            kts.append(jnp.where(m, k, 0.0).astype(_BF16))
            vs.append(pcur(_GV + idx * GLA_DV, _GV + (idx + 1) * GLA_DV).astype(_BF16))
            ebs.append(eb)
    for hd in range(HG_HEADS):
        b = b_all[:, (2 + hd) * LANES:(3 + hd) * LANES]
        eb = jnp.exp(b)
        hq = pcur(_HQ + hd * LANES, _HQ + (hd + 1) * LANES)
        qts.append((hq * _sigmoid(hq) * eb).astype(_BF16))
        kts.append((k_in[hd] * jnp.exp(-b)).astype(_BF16))
        vs.append(pcur(_HI + hd * LANES, _HI + (hd + 1) * LANES).astype(_BF16))
        ebs.append(eb)

    gs = min(tm, LANES)
    groups = [slice(u * gs, (u + 1) * gs) for u in range(tm // gs)]
    blocks = [slice(c * blk, (c + 1) * blk) for c in range(nb)]
    pairs = [(2 * p, 2 * p + 1) for p in range(N_HEADS // 2)]
    row2 = lax.broadcasted_iota(jnp.int32, (gs, 2 * gs), 0)
    col2 = lax.broadcasted_iota(jnp.int32, (gs, 2 * gs), 1)
    col2 = jnp.where(col2 >= gs, col2 - gs, col2)
    causal2 = (col2 <= row2) & (col2 >= row2 - row2 % blk)

    q2 = [cat(qts[a], qts[b], 1) for a, b in pairs]
    fill(1)
    scores = [[lax.dot_general(q2[p][g], bdiag(kts[a][g], kts[b][g]), _NT, preferred_element_type=_F32)
               for g in groups] for p, (a, b) in enumerate(pairs)]
    kvs = [[lax.dot_general(cat(vs[a][rs], vs[b][rs], 0), bdiag(kts[a][rs], kts[b][rs]), _TN,
                            preferred_element_type=_F32)
            for rs in blocks] for a, b in pairs]
    yield
    fill(2)
    o_intra = [[jnp.dot(jnp.where(causal2, scores[p][u], 0.0).astype(_BF16), bdiag(vs[a][g], vs[b][g]),
                        preferred_element_type=_F32)
                for u, g in enumerate(groups)] for p, (a, b) in enumerate(pairs)]
    yield
    fill(3)
    states = [cat(st_ref[a], st_ref[b], 1) for a, b in pairs]
    e2 = [cat(ebs[a], ebs[b], 1) for a, b in pairs]
    o_inter = [[] for _ in pairs]
    for c, rs in enumerate(blocks):
        for p in range(len(pairs)):
            s2 = states[p].astype(_BF16)
            o_inter[p].append(lax.dot_general(q2[p][rs], bdiag(s2[:, :LANES], s2[:, LANES:]), _NT,
                                              preferred_element_type=_F32))
        for p in range(len(pairs)):
            states[p] = (states[p] + kvs[p][c]) * e2[p][(c + 1) * blk - 1:(c + 1) * blk, :]
    yield
    fill(4)
    for p, (a, b) in enumerate(pairs):
        st_ref[a] = states[p][:, :LANES]
        st_ref[b] = states[p][:, LANES:]
        o2 = cat_rows(o_intra[p]) + cat_rows(o_inter[p])
        for hh, h in enumerate((a, b)):
            o = o2[:, hh * LANES:(hh + 1) * LANES]
            onorm = gon_ref[...] if h < GLA_HEADS else hon_ref[...]
            gate_col = _GR + h * LANES if h < GLA_HEADS else _HGATE + (h - GLA_HEADS) * LANES
            gate = pcur(gate_col, gate_col + LANES)
            o = o * lax.rsqrt(jnp.mean(o * o, axis=-1, keepdims=True) + EPS) * onorm
            o = o * (gate * _sigmoid(gate))
            mix_ref[rows, h * LANES:(h + 1) * LANES] = o.astype(_BF16)
    yield
    fill(5)
    out = jnp.dot(mix_ref[rows, :], wout_ref[...], preferred_element_type=_F32)
    fill(6)
    o_ref[rows, :] = x_ref[rows, :] + out


def _mixer_kernel(x_ref, xn_ref, st_in_ref, g_ref, win_ref, gup_ref, gbias_ref, lb_ref, gon_ref, hon_ref, wout_ref,
                  o_ref, st_ref, pa_ref, pb_ref, ha_ref, hb_ref, mix_ref, *, tm, blk, pipelined):
    u = pl.program_id(1)
    tile = functools.partial(_mixer_tile, x_ref=x_ref, o_ref=o_ref, mix_ref=mix_ref, st_ref=st_ref,
                             gup_ref=gup_ref, gbias_ref=gbias_ref, lb_ref=lb_ref, gon_ref=gon_ref,
                             hon_ref=hon_ref, wout_ref=wout_ref, tm=tm, blk=blk)

    def normed(src_ref, r0):
        return _rmsnorm(src_ref[r0:r0 + tm, :], g_ref[...]).astype(_BF16)

    valid = {id(pa_ref): set(range(N_PROJ_PIECES)), id(pb_ref): set()}

    def project(h, dst_ref, pieces, for_reader):
        for i in pieces:
            cols = slice(i * PROJ_PIECE, min((i + 1) * PROJ_PIECE, D_PROJ))
            dst_ref[:, cols] = jnp.dot(h, win_ref[:, cols], preferred_element_type=_F32)
            (valid[id(dst_ref)].add if for_reader else valid[id(dst_ref)].discard)(i)

    def reader(buf_ref):
        def read(c0, c1):
            assert all(i in valid[id(buf_ref)] for i in range(c0 // PROJ_PIECE, (c1 - 1) // PROJ_PIECE + 1))
            return buf_ref[:, c0:c1]
        return read

    def filler(h_ref, dst_ref, schedule, for_reader):
        pieces = iter(range(N_PROJ_PIECES))
        return lambda stage: project(h_ref[...], dst_ref, [next(pieces) for _ in range(schedule[stage])],
                                     for_reader)

    @pl.when(u == 0)
    def _():
        for h in range(N_HEADS):
            st_ref[h] = st_in_ref[h].T

    if pipelined:
        @pl.when((pl.program_id(0) == 0) & (u == 0))
        def _():
            project(normed(x_ref, 0), pa_ref, range(N_PROJ_PIECES), True)
            hb_ref[...] = normed(x_ref, tm)

        ha_ref[...] = normed(xn_ref, 0)
        stages_a = tile(reader(pa_ref), r0=0, fill=filler(hb_ref, pb_ref, PROJ_SCHEDULE_A, True))
        stages_b = tile(reader(pb_ref), r0=tm, fill=filler(ha_ref, pa_ref, PROJ_SCHEDULE_B, False))
        for turn in TILE_ORDER:
            next(stages_a if turn == "a" else stages_b, None)
        hb_ref[...] = normed(xn_ref, tm)
    else:
        project(normed(x_ref, 0), pa_ref, range(N_PROJ_PIECES), True)
        for _ in tile(reader(pa_ref), r0=0, fill=lambda stage: None):
            pass

    @pl.when(u == pl.num_programs(1) - 1)
    def _():
        for h in range(N_HEADS):
            st_ref[h] = st_ref[h].T


def _mixer(x, st_in, g, win, gup, gbias, lb, gon, hon, wout, *, tm, blk):
    bsz, seq, _ = x.shape
    pipelined = seq % (2 * tm) == 0
    rows = 2 * tm if pipelined else tm
    assert seq % rows == 0 and tm % blk == 0
    nu = seq // rows
    last = bsz * nu - 1

    def next_block(b, u):
        nxt = jnp.minimum(b * nu + u + 1, last)
        return (nxt // nu, nxt % nu, 0)

    row_spec = pl.BlockSpec((None, rows, D_MODEL), lambda b, u: (b, u, 0))
    st_spec = pl.BlockSpec((None, N_HEADS, LANES, LANES), lambda b, u: (b, 0, 0, 0))
    return pl.pallas_call(
        functools.partial(_mixer_kernel, tm=tm, blk=blk, pipelined=pipelined),
        grid=(bsz, nu),
        in_specs=[row_spec, pl.BlockSpec((None, rows, D_MODEL), next_block), st_spec]
        + [_const_spec(a.shape) for a in (g, win, gup, gbias, lb, gon, hon, wout)],
        out_specs=[row_spec, st_spec],
        out_shape=[jax.ShapeDtypeStruct(x.shape, _F32),
                   jax.ShapeDtypeStruct((bsz, N_HEADS, LANES, LANES), _F32)],
        scratch_shapes=[pltpu.VMEM((tm, D_PROJ), _F32), pltpu.VMEM((tm, D_PROJ), _F32),
                        pltpu.VMEM((tm, D_MODEL), _BF16), pltpu.VMEM((tm, D_MODEL), _BF16),
                        pltpu.VMEM((rows, D_MODEL), _BF16)],
        compiler_params=pltpu.CompilerParams(dimension_semantics=("arbitrary", "arbitrary"),
                                             vmem_limit_bytes=VMEM_LIMIT),
        name="mixer",
    )(x, x, st_in, g, win, gup, gbias, lb, gon, hon, wout)


def _pack_state(s_gla, s_hg):
    zeros = jnp.zeros_like(s_gla)
    even = jnp.concatenate([s_gla, zeros], axis=2)
    odd = jnp.concatenate([zeros, s_gla], axis=2)
    is_odd = (jnp.arange(GLA_HEADS) % 2 == 1)[None, :, None, None]
    gla = jnp.where(is_odd, odd, even)
    return jnp.concatenate([gla, s_hg], axis=1).astype(_F32)


def _unpack_state(st):
    gla = st[:, :GLA_HEADS]
    is_odd = (jnp.arange(GLA_HEADS) % 2 == 1)[None, :, None, None]
    gla = jnp.where(is_odd, gla[:, :, GLA_DK:], gla[:, :, :GLA_DK])
    return gla, st[:, GLA_HEADS:]


def kernel(x_prompt, x_sample, state_gla, state_hgrn, norm_ffn1, ffn1_w_gate, ffn1_w_up, ffn1_w_down, norm_mix, w_in, gla_gate_up, gla_gate_bias, gla_onorm, hg_lower_bound_logits, hg_onorm, w_out, norm_ffn2, ffn2_w_gate, ffn2_w_up, ffn2_w_down, norm_final):
    assert norm_ffn1.shape[0] == 1, "single layer"
    row = lambda v: v.reshape(1, -1).astype(_F32)
    bf = lambda w: w.astype(_BF16)
    f1 = (row(norm_ffn1[0]), bf(ffn1_w_gate[0]), bf(ffn1_w_up[0]), bf(ffn1_w_down[0]), row(norm_final))
    f2 = (row(norm_ffn2[0]), bf(ffn2_w_gate[0]), bf(ffn2_w_up[0]), bf(ffn2_w_down[0]), row(norm_final))
    sizes = (GLA_HEADS * GLA_DK, GLA_HEADS * GLA_DK, GLA_HEADS * GLA_DV, GLA_HEADS * GLA_DV, GLA_GATE_RANK,
             HG_HEADS * HG_EXPAND, HG_HEADS * HG_EXPAND, HG_HEADS * HG_HEAD_DIM, HG_HEADS * HG_HEAD_DIM)
    splits = [sum(sizes[:i + 1]) for i in range(len(sizes) - 1)]
    gq, gk, gv, gr, gd, hq, hf, hi, hgate = jnp.split(w_in[0], splits, axis=1)
    gd = jnp.concatenate([gd, jnp.zeros((D_MODEL, _HF - GLA_GATE_RANK), gd.dtype)], axis=1)
    win = jnp.concatenate([gd, hf, gq, gk, hq, gv, hi, gr, hgate], axis=1).astype(_BF16)
    gup = jnp.concatenate([gla_gate_up[0], jnp.zeros((LANES - GLA_GATE_RANK, GLA_HEADS * GLA_DK), _F32)],
                          axis=0).astype(_BF16)
    lb = jnp.cumsum(jax.nn.softmax(hg_lower_bound_logits.astype(_F32), axis=0), axis=0)[0]
    mix_w = (row(norm_mix[0]), win, gup, row(gla_gate_bias[0]), row(lb), row(gla_onorm[0]), row(hg_onorm[0]),
             bf(w_out[0]))

    def trunk(x, s_gla, s_hg, tm_ffn, tm_mix, blk):
        bsz, seq, _ = x.shape
        x1 = _ffn(x.reshape(bsz * seq, D_MODEL), *f1, tm=tm_ffn, final_norm=False)
        x2, st = _mixer(x1.reshape(bsz, seq, D_MODEL), _pack_state(s_gla, s_hg), *mix_w, tm=tm_mix, blk=blk)
        y = _ffn(x2.reshape(bsz * seq, D_MODEL), *f2, tm=tm_ffn, final_norm=True)
        g, h = _unpack_state(st)
        return y.reshape(bsz, seq, D_MODEL), g[None], h[None]

    bp, sp, _ = x_prompt.shape
    bs, ss, _ = x_sample.shape
    zg = jnp.zeros((bp, GLA_HEADS, GLA_DK, GLA_DV), _F32)
    zh = jnp.zeros((bp, HG_HEADS, HG_EXPAND, HG_HEAD_DIM), _F32)
    blk_p = min(CHUNK, sp)
    y_p, g_p, h_p = trunk(x_prompt, zg, zh, FFN_ROWS, min(MIX_ROWS, sp), blk_p)
    y_s, g_s, h_s = trunk(x_sample, state_gla[0], state_hgrn[0], bs * ss, ss, ss)
    sd_g, sd_h = state_gla.dtype, state_hgrn.dtype
    return (y_p, y_s, g_p.astype(sd_g), h_p.astype(sd_h), g_s.astype(sd_g), h_s.astype(sd_h))
```

```python
import functools

import jax
import jax.numpy as jnp
from jax import lax
from jax.experimental import pallas as pl
from jax.experimental.pallas import tpu as pltpu

D_MODEL = 1024
CHUNK = 64
GLA_HEADS = 4
GLA_DK = 64
GLA_DV = 128
GLA_GATE_RANK = 16
GLA_GATE_NORM = 16.0
HG_HEADS = 4
HG_EXPAND = 128
HG_HEAD_DIM = 128
D_FF = 2816
EPS = 1e-6

LANES = 128
MXU_COLS = 256
N_HEADS = GLA_HEADS + HG_HEADS
FF_CHUNK = MXU_COLS
N_FF_CHUNKS = D_FF // FF_CHUNK
FFN_ROWS = 512
MIX_ROWS = 256
VMEM_LIMIT = 56 * 1024 * 1024
DECAY_GUARD = 60.0

_GD, _HF, _GQ, _GK, _HQ, _GV, _HI, _GR, _HGATE = 0, 256, 768, 1024, 1280, 1792, 2304, 2816, 3328
D_PROJ = _HGATE + HG_HEADS * HG_HEAD_DIM
PROJ_PIECE = 2 * MXU_COLS
N_PROJ_PIECES = -(-D_PROJ // PROJ_PIECE)
PROJ_SCHEDULE = (2, 2, 1, 1, 0, 1, 1)
assert sum(PROJ_SCHEDULE) == N_PROJ_PIECES

_F32 = jnp.float32
_BF16 = jnp.bfloat16


def _rmsnorm(x, g):
    return x * lax.rsqrt(jnp.mean(x * x, axis=-1, keepdims=True) + EPS) * g


def _sigmoid(x):
    return 1.0 / (1.0 + jnp.exp(-x))


def _dot(a, b):
    return jnp.dot(a, b, preferred_element_type=_F32)


_NT = (((1,), (1,)), ((), ()))
_TN = (((0,), (0,)), ((), ()))


def _dot_exact_rhs(sel, x):
    hi = x.astype(_BF16)
    r1 = x - hi.astype(_F32)
    mid = r1.astype(_BF16)
    lo = (r1 - mid.astype(_F32)).astype(_BF16)
    n = x.shape[1]
    parts = jnp.concatenate([hi, mid, lo], axis=1)
    r = jnp.dot(sel, parts, preferred_element_type=_F32)
    return r[:, :n] + r[:, n:2 * n] + r[:, 2 * n:]


def _ffn_kernel(x_ref, g_ref, wg_ref, wu_ref, wd_ref, gf_ref, o_ref, acc_ref, *, final_norm):
    x = x_ref[...]
    h = _rmsnorm(x, g_ref[...]).astype(_BF16)
    for c in range(N_FF_CHUNKS):
        cols = slice(c * FF_CHUNK, (c + 1) * FF_CHUNK)
        gate = _dot(h, wg_ref[:, cols])
        up = _dot(h, wu_ref[:, cols])
        a = (gate * _sigmoid(gate) * up).astype(_BF16)
        d = _dot(a, wd_ref[cols, :])
        if c == 0:
            acc_ref[...] = d
        else:
            acc_ref[...] += d
    y = x + 0.5 * acc_ref[...]
    if final_norm:
        y = _rmsnorm(y, gf_ref[...])
    o_ref[...] = y


def _const_spec(shape):
    zeros = (0,) * len(shape)
    return pl.BlockSpec(shape, lambda *_: zeros, pipeline_mode=pl.Buffered(1))


def _ffn(x2d, g, wg, wu, wd, gf, *, tm, final_norm):
    n = x2d.shape[0]
    assert n % tm == 0
    row_spec = pl.BlockSpec((tm, D_MODEL), lambda i: (i, 0))
    return pl.pallas_call(
        functools.partial(_ffn_kernel, final_norm=final_norm),
        grid=(n // tm,),
        in_specs=[row_spec, _const_spec(g.shape), _const_spec(wg.shape), _const_spec(wu.shape),
                  _const_spec(wd.shape), _const_spec(gf.shape)],
        out_specs=row_spec,
        out_shape=jax.ShapeDtypeStruct((n, D_MODEL), _F32),
        scratch_shapes=[pltpu.VMEM((tm, D_MODEL), _F32)],
        compiler_params=pltpu.CompilerParams(dimension_semantics=("arbitrary",),
                                             vmem_limit_bytes=VMEM_LIMIT),
        name="ffn_final" if final_norm else "ffn",
    )(x2d, g, wg, wu, wd, gf)


def _safe_scores(q, k, b, blk):
    gs = q.shape[0]
    ri = lax.broadcasted_iota(jnp.int32, (gs, gs), 0)
    ci = lax.broadcasted_iota(jnp.int32, (gs, gs), 1)
    qk = lax.dot_general(q.astype(_BF16), k.astype(_BF16), _NT, preferred_element_type=_F32)
    total = jnp.where(ri == ci, qk, 0.0)
    s = 1
    while s < blk:
        run = ri - ri % (2 * s)
        b_ref_rows = _dot_exact_rhs((ci == run + (s - 1)).astype(_BF16), b)
        qs = (q * jnp.exp(jnp.minimum(b - b_ref_rows, 0.0))).astype(_BF16)
        ks = (k * jnp.exp(jnp.minimum(b_ref_rows - b, 0.0))).astype(_BF16)
        level = (ri % (2 * s) >= s) & (ci % (2 * s) < s) & (run == ci - ci % (2 * s))
        total = total + jnp.where(level, lax.dot_general(qs, ks, _NT, preferred_element_type=_F32), 0.0)
        s *= 2
    return total


def _mixer_tile(pcur, st_from, st_to, x_ref, o_ref, mix_ref, gup_ref, gbias_ref, lb_ref, gon_ref, hon_ref, wout_ref,
                *, r0, tm, blk, fill, safe):
    nb = tm // blk
    rows = slice(r0, r0 + tm)
    row = lax.broadcasted_iota(jnp.int32, (tm, tm), 0)
    col = lax.broadcasted_iota(jnp.int32, (tm, tm), 1)
    tri = ((col <= row) & (col >= row - row % blk)).astype(_BF16)
    lane = lax.broadcasted_iota(jnp.int32, (tm, LANES), 1)
    lb = lb_ref[...]

    def cat(x, y, axis):
        return jnp.concatenate([x, y], axis=axis)

    def cat_rows(parts):
        return parts[0] if len(parts) == 1 else jnp.concatenate(parts, axis=0)

    def bdiag(x, y):
        return cat(cat(x, jnp.zeros_like(y), 1), cat(jnp.zeros_like(x), y, 1), 0)

    gd = pcur(_GD, _GD + LANES).astype(_BF16)
    logits = jnp.dot(gd, gup_ref[...], preferred_element_type=_F32) + gbias_ref[...]
    g_cols = [(jnp.minimum(logits, 0.0) - jnp.log1p(jnp.exp(-jnp.abs(logits)))) / GLA_GATE_NORM]
    k_in = []
    for hd in range(HG_HEADS):
        z = pcur(_HF + hd * LANES, _HF + (hd + 1) * LANES)
        lb_h = lb[:, hd * LANES:(hd + 1) * LANES]
        e = jnp.exp(-jnp.abs(z))
        r = 1.0 / (1.0 + e)
        sig_pos = jnp.where(z >= 0, r, e * r)
        sig_neg = jnp.where(z >= 0, e * r, r)
        g_cols.append(jnp.log(lb_h + (1.0 - lb_h) * sig_pos))
        k_in.append((1.0 - lb_h) * sig_neg)
    fill(0)
    b_all = _dot_exact_rhs(tri, jnp.concatenate(g_cols, axis=1))
    steep = jnp.min(b_all) < -DECAY_GUARD

    def block_last(b):
        return cat_rows([jnp.broadcast_to(b[(c + 1) * blk - 1:(c + 1) * blk, :], (blk, LANES)) for c in range(nb)])

    qs, ks, bs, qts, khs, vs, ebs = [], [], [], [], [], [], []
    for p in range(GLA_HEADS // 2):
        b = b_all[:, p * LANES:(p + 1) * LANES]
        eb = jnp.exp(b)
        eh = jnp.exp(block_last(b) - b)
        q = pcur(_GQ + p * LANES, _GQ + (p + 1) * LANES) * (GLA_DK ** -0.5)
        k = pcur(_GK + p * LANES, _GK + (p + 1) * LANES)
        for hh in range(2):
            idx = 2 * p + hh
            m = (lane >= hh * GLA_DK) & (lane < (hh + 1) * GLA_DK)
            qs.append(jnp.where(m, q, 0.0))
            ks.append(jnp.where(m, k, 0.0))
            bs.append(b)
            qts.append((qs[-1] * eb).astype(_BF16))
            khs.append((ks[-1] * eh).astype(_BF16))
            vs.append(pcur(_GV + idx * GLA_DV, _GV + (idx + 1) * GLA_DV).astype(_BF16))
            ebs.append(eb)
    for hd in range(HG_HEADS):
        b = b_all[:, (2 + hd) * LANES:(3 + hd) * LANES]
        eb = jnp.exp(b)
        hq = pcur(_HQ + hd * LANES, _HQ + (hd + 1) * LANES)
        qs.append(hq * _sigmoid(hq))
        ks.append(k_in[hd])
        bs.append(b)
        qts.append((qs[-1] * eb).astype(_BF16))
        khs.append((ks[-1] * jnp.exp(block_last(b) - b)).astype(_BF16))
        vs.append(pcur(_HI + hd * LANES, _HI + (hd + 1) * LANES).astype(_BF16))
        ebs.append(eb)

    gs = min(tm, LANES)
    groups = [slice(u * gs, (u + 1) * gs) for u in range(tm // gs)]
    blocks = [slice(c * blk, (c + 1) * blk) for c in range(nb)]
    pairs = [(2 * p, 2 * p + 1) for p in range(N_HEADS // 2)]
    row2 = lax.broadcasted_iota(jnp.int32, (gs, 2 * gs), 0)
    col2 = lax.broadcasted_iota(jnp.int32, (gs, 2 * gs), 1)
    col2 = jnp.where(col2 >= gs, col2 - gs, col2)
    causal2 = (col2 <= row2) & (col2 >= row2 - row2 % blk)

    q2 = [cat(qts[a], qts[b], 1) for a, b in pairs]
    fill(1)
    kvs = [[lax.dot_general(cat(vs[a][rs], vs[b][rs], 0), bdiag(khs[a][rs], khs[b][rs]), _TN,
                            preferred_element_type=_F32)
            for rs in blocks] for a, b in pairs]

    if safe:
        scores = [[cat(_safe_scores(qs[a][g], ks[a][g], bs[a][g], blk),
                       _safe_scores(qs[b][g], ks[b][g], bs[b][g], blk), 1) for g in groups] for a, b in pairs]
    else:
        kts = [(ks[h] * jnp.exp(-bs[h])).astype(_BF16) for h in range(N_HEADS)]
        scores = [[lax.dot_general(q2[p][g], bdiag(kts[a][g], kts[b][g]), _NT, preferred_element_type=_F32)
                   for g in groups] for p, (a, b) in enumerate(pairs)]
    fill(2)
    o_intra = [[jnp.dot(jnp.where(causal2, scores[p][u], 0.0).astype(_BF16), bdiag(vs[a][g], vs[b][g]),
                        preferred_element_type=_F32)
                for u, g in enumerate(groups)] for p, (a, b) in enumerate(pairs)]
    fill(3)
    states = [cat(st_from[a], st_from[b], 1) for a, b in pairs]
    e2 = [cat(ebs[a], ebs[b], 1) for a, b in pairs]
    o_inter = [[] for _ in pairs]
    for c, rs in enumerate(blocks):
        for p in range(len(pairs)):
            s2 = states[p].astype(_BF16)
            o_inter[p].append(lax.dot_general(q2[p][rs], bdiag(s2[:, :LANES], s2[:, LANES:]), _NT,
                                              preferred_element_type=_F32))
        for p in range(len(pairs)):
            states[p] = states[p] * e2[p][(c + 1) * blk - 1:(c + 1) * blk, :] + kvs[p][c]
    fill(4)
    for p, (a, b) in enumerate(pairs):
        st_to[a] = states[p][:, :LANES]
        st_to[b] = states[p][:, LANES:]
        o2 = cat_rows(o_intra[p]) + cat_rows(o_inter[p])
        for hh, h in enumerate((a, b)):
            o = o2[:, hh * LANES:(hh + 1) * LANES]
            onorm = gon_ref[...] if h < GLA_HEADS else hon_ref[...]
            gate_col = _GR + h * LANES if h < GLA_HEADS else _HGATE + (h - GLA_HEADS) * LANES
            gate = pcur(gate_col, gate_col + LANES)
            o = o * lax.rsqrt(jnp.mean(o * o, axis=-1, keepdims=True) + EPS) * onorm
            o = o * (gate * _sigmoid(gate))
            mix_ref[rows, h * LANES:(h + 1) * LANES] = o.astype(_BF16)
    fill(5)
    out = jnp.dot(mix_ref[rows, :], wout_ref[...], preferred_element_type=_F32)
    fill(6)
    o_ref[rows, :] = x_ref[rows, :] + out
    return steep


def _mixer_kernel(x_ref, xn_ref, st_in_ref, g_ref, win_ref, gup_ref, gbias_ref, lb_ref, gon_ref, hon_ref, wout_ref,
                  o_ref, st_ref, pa_ref, pb_ref, ha_ref, hb_ref, mix_ref, s1_ref, s2_ref, *, tm, blk, pipelined):
    u = pl.program_id(1)
    tile = functools.partial(_mixer_tile, x_ref=x_ref, o_ref=o_ref, mix_ref=mix_ref, gup_ref=gup_ref,
                             gbias_ref=gbias_ref, lb_ref=lb_ref, gon_ref=gon_ref, hon_ref=hon_ref,
                             wout_ref=wout_ref, tm=tm, blk=blk)
    no_fill = lambda stage: None

    def normed(src_ref, r0):
        return _rmsnorm(src_ref[r0:r0 + tm, :], g_ref[...]).astype(_BF16)

    valid = {id(pa_ref): set(range(N_PROJ_PIECES)), id(pb_ref): set()}

    def project(h, dst_ref, pieces, for_reader):
        for i in pieces:
            cols = slice(i * PROJ_PIECE, min((i + 1) * PROJ_PIECE, D_PROJ))
            dst_ref[:, cols] = jnp.dot(h, win_ref[:, cols], preferred_element_type=_F32)
            (valid[id(dst_ref)].add if for_reader else valid[id(dst_ref)].discard)(i)

    def reader(buf_ref):
        def read(c0, c1):
            assert all(i in valid[id(buf_ref)] for i in range(c0 // PROJ_PIECE, (c1 - 1) // PROJ_PIECE + 1))
            return buf_ref[:, c0:c1]
        return read

    def filler(h_ref, dst_ref, for_reader):
        pieces = iter(range(N_PROJ_PIECES))
        return lambda stage: project(h_ref[...], dst_ref, [next(pieces) for _ in range(PROJ_SCHEDULE[stage])],
                                     for_reader)

    @pl.when(u == 0)
    def _():
        for h in range(N_HEADS):
            st_ref[h] = st_in_ref[h].T

    if pipelined:
        @pl.when((pl.program_id(0) == 0) & (u == 0))
        def _():
            project(normed(x_ref, 0), pa_ref, range(N_PROJ_PIECES), True)
            hb_ref[...] = normed(x_ref, tm)

        ha_ref[...] = normed(xn_ref, 0)
        steep_a = tile(reader(pa_ref), st_ref, s1_ref, r0=0, fill=filler(hb_ref, pb_ref, True), safe=False)
        steep_b = tile(reader(pb_ref), s1_ref, s2_ref, r0=tm, fill=filler(ha_ref, pa_ref, False), safe=False)
        hb_ref[...] = normed(xn_ref, tm)

        @pl.when(steep_b)
        def _():
            tile(reader(pb_ref), s1_ref, s2_ref, r0=tm, fill=no_fill, safe=True)

        @pl.when(steep_a)
        def _():
            project(normed(x_ref, 0), pb_ref, range(N_PROJ_PIECES), True)
            tile(reader(pb_ref), st_ref, s1_ref, r0=0, fill=no_fill, safe=True)

        final = s2_ref
    else:
        project(normed(x_ref, 0), pa_ref, range(N_PROJ_PIECES), True)
        steep_a = tile(reader(pa_ref), st_ref, s1_ref, r0=0, fill=no_fill, safe=False)

        @pl.when(steep_a)
        def _():
            tile(reader(pa_ref), st_ref, s1_ref, r0=0, fill=no_fill, safe=True)

        final = s1_ref

    for h in range(N_HEADS):
        st_ref[h] = final[h]

    @pl.when(u == pl.num_programs(1) - 1)
    def _():
        for h in range(N_HEADS):
            st_ref[h] = st_ref[h].T


def _mixer(x, st_in, g, win, gup, gbias, lb, gon, hon, wout, *, tm, blk):
    bsz, seq, _ = x.shape
    pipelined = seq % (2 * tm) == 0
    rows = 2 * tm if pipelined else tm
    assert seq % rows == 0 and tm % blk == 0 and blk & (blk - 1) == 0
    nu = seq // rows
    last = bsz * nu - 1

    def next_block(b, u):
        nxt = jnp.minimum(b * nu + u + 1, last)
        return (nxt // nu, nxt % nu, 0)

    row_spec = pl.BlockSpec((None, rows, D_MODEL), lambda b, u: (b, u, 0))
    st_spec = pl.BlockSpec((None, N_HEADS, LANES, LANES), lambda b, u: (b, 0, 0, 0))
    return pl.pallas_call(
        functools.partial(_mixer_kernel, tm=tm, blk=blk, pipelined=pipelined),
        grid=(bsz, nu),
        in_specs=[row_spec, pl.BlockSpec((None, rows, D_MODEL), next_block), st_spec]
        + [_const_spec(a.shape) for a in (g, win, gup, gbias, lb, gon, hon, wout)],
        out_specs=[row_spec, st_spec],
        out_shape=[jax.ShapeDtypeStruct(x.shape, _F32),
                   jax.ShapeDtypeStruct((bsz, N_HEADS, LANES, LANES), _F32)],
        scratch_shapes=[pltpu.VMEM((tm, D_PROJ), _F32), pltpu.VMEM((tm, D_PROJ), _F32),
                        pltpu.VMEM((tm, D_MODEL), _BF16), pltpu.VMEM((tm, D_MODEL), _BF16),
                        pltpu.VMEM((rows, D_MODEL), _BF16),
                        pltpu.VMEM((N_HEADS, LANES, LANES), _F32), pltpu.VMEM((N_HEADS, LANES, LANES), _F32)],
        compiler_params=pltpu.CompilerParams(dimension_semantics=("arbitrary", "arbitrary"),
                                             vmem_limit_bytes=VMEM_LIMIT),
        name="mixer",
    )(x, x, st_in, g, win, gup, gbias, lb, gon, hon, wout)


def _pack_state(s_gla, s_hg):
    zeros = jnp.zeros_like(s_gla)
    even = jnp.concatenate([s_gla, zeros], axis=2)
    odd = jnp.concatenate([zeros, s_gla], axis=2)
    is_odd = (jnp.arange(GLA_HEADS) % 2 == 1)[None, :, None, None]
    gla = jnp.where(is_odd, odd, even)
    return jnp.concatenate([gla, s_hg], axis=1).astype(_F32)


def _unpack_state(st):
    gla = st[:, :GLA_HEADS]
    is_odd = (jnp.arange(GLA_HEADS) % 2 == 1)[None, :, None, None]
    gla = jnp.where(is_odd, gla[:, :, GLA_DK:], gla[:, :, :GLA_DK])
    return gla, st[:, GLA_HEADS:]


def kernel(x_prompt, x_sample, state_gla, state_hgrn, norm_ffn1, ffn1_w_gate, ffn1_w_up, ffn1_w_down, norm_mix, w_in, gla_gate_up, gla_gate_bias, gla_onorm, hg_lower_bound_logits, hg_onorm, w_out, norm_ffn2, ffn2_w_gate, ffn2_w_up, ffn2_w_down, norm_final):
    assert norm_ffn1.shape[0] == 1, "single layer"
    row = lambda v: v.reshape(1, -1).astype(_F32)
    bf = lambda w: w.astype(_BF16)
    f1 = (row(norm_ffn1[0]), bf(ffn1_w_gate[0]), bf(ffn1_w_up[0]), bf(ffn1_w_down[0]), row(norm_final))
    f2 = (row(norm_ffn2[0]), bf(ffn2_w_gate[0]), bf(ffn2_w_up[0]), bf(ffn2_w_down[0]), row(norm_final))
    sizes = (GLA_HEADS * GLA_DK, GLA_HEADS * GLA_DK, GLA_HEADS * GLA_DV, GLA_HEADS * GLA_DV, GLA_GATE_RANK,
             HG_HEADS * HG_EXPAND, HG_HEADS * HG_EXPAND, HG_HEADS * HG_HEAD_DIM, HG_HEADS * HG_HEAD_DIM)
    splits = [sum(sizes[:i + 1]) for i in range(len(sizes) - 1)]
    gq, gk, gv, gr, gd, hq, hf, hi, hgate = jnp.split(w_in[0], splits, axis=1)
    gd = jnp.concatenate([gd, jnp.zeros((D_MODEL, _HF - GLA_GATE_RANK), gd.dtype)], axis=1)
    win = jnp.concatenate([gd, hf, gq, gk, hq, gv, hi, gr, hgate], axis=1).astype(_BF16)
    gup = jnp.concatenate([gla_gate_up[0], jnp.zeros((LANES - GLA_GATE_RANK, GLA_HEADS * GLA_DK), _F32)],
                          axis=0).astype(_BF16)
    lb = jnp.cumsum(jax.nn.softmax(hg_lower_bound_logits.astype(_F32), axis=0), axis=0)[0]
    mix_w = (row(norm_mix[0]), win, gup, row(gla_gate_bias[0]), row(lb), row(gla_onorm[0]), row(hg_onorm[0]),
             bf(w_out[0]))

    def trunk(x, s_gla, s_hg, tm_ffn, tm_mix, blk):
        bsz, seq, _ = x.shape
        x1 = _ffn(x.reshape(bsz * seq, D_MODEL), *f1, tm=tm_ffn, final_norm=False)
        x2, st = _mixer(x1.reshape(bsz, seq, D_MODEL), _pack_state(s_gla, s_hg), *mix_w, tm=tm_mix, blk=blk)
        y = _ffn(x2.reshape(bsz * seq, D_MODEL), *f2, tm=tm_ffn, final_norm=True)
        g, h = _unpack_state(st)
        return y.reshape(bsz, seq, D_MODEL), g[None], h[None]

    bp, sp, _ = x_prompt.shape
    bs, ss, _ = x_sample.shape
    zg = jnp.zeros((bp, GLA_HEADS, GLA_DK, GLA_DV), _F32)
    zh = jnp.zeros((bp, HG_HEADS, HG_EXPAND, HG_HEAD_DIM), _F32)
    blk_p = min(CHUNK, sp)
    y_p, g_p, h_p = trunk(x_prompt, zg, zh, FFN_ROWS, min(MIX_ROWS, sp), blk_p)
    y_s, g_s, h_s = trunk(x_sample, state_gla[0], state_hgrn[0], bs * ss, ss, ss)
    sd_g, sd_h = state_gla.dtype, state_hgrn.dtype
    return (y_p, y_s, g_p.astype(sd_g), h_p.astype(sd_h), g_s.astype(sd_g), h_s.astype(sd_h))
```

```python
import functools

import jax
import jax.numpy as jnp
from jax import lax
from jax.experimental import pallas as pl
from jax.experimental.pallas import tpu as pltpu

D_MODEL = 1024
CHUNK = 64
GLA_HEADS = 4
GLA_DK = 64
GLA_DV = 128
GLA_GATE_RANK = 16
GLA_GATE_NORM = 16.0
HG_HEADS = 4
HG_EXPAND = 128
HG_HEAD_DIM = 128
D_FF = 2816
EPS = 1e-6

LANES = 128
MXU_COLS = 256
N_HEADS = GLA_HEADS + HG_HEADS
FF_CHUNK = MXU_COLS
N_FF_CHUNKS = D_FF // FF_CHUNK
FFN_ROWS = 512
MIX_ROWS = 256
VMEM_LIMIT = 56 * 1024 * 1024
DECAY_GUARD = 60.0

_GD, _HF, _GQ, _GK, _HQ, _GV, _HI, _GR, _HGATE = 0, 256, 768, 1024, 1280, 1792, 2304, 2816, 3328
D_PROJ = _HGATE + HG_HEADS * HG_HEAD_DIM
PROJ_PIECE = 2 * MXU_COLS
N_PROJ_PIECES = -(-D_PROJ // PROJ_PIECE)
PROJ_SCHEDULE = (2, 2, 1, 1, 0, 1, 1)
assert sum(PROJ_SCHEDULE) == N_PROJ_PIECES

_F32 = jnp.float32
_BF16 = jnp.bfloat16


def _rmsnorm(x, g):
    return x * lax.rsqrt(jnp.mean(x * x, axis=-1, keepdims=True) + EPS) * g


def _sigmoid(x):
    return 1.0 / (1.0 + jnp.exp(-x))


def _dot(a, b):
    return jnp.dot(a, b, preferred_element_type=_F32)


_NT = (((1,), (1,)), ((), ()))
_TN = (((0,), (0,)), ((), ()))


def _dot_exact_rhs(sel, x):
    hi = x.astype(_BF16)
    r1 = x - hi.astype(_F32)
    mid = r1.astype(_BF16)
    lo = (r1 - mid.astype(_F32)).astype(_BF16)
    n = x.shape[1]
    parts = jnp.concatenate([hi, mid, lo], axis=1)
    r = jnp.dot(sel, parts, preferred_element_type=_F32)
    return r[:, :n] + r[:, n:2 * n] + r[:, 2 * n:]


def _ffn_kernel(x_ref, g_ref, wg_ref, wu_ref, wd_ref, gf_ref, o_ref, acc_ref, *, final_norm):
    x = x_ref[...]
    h = _rmsnorm(x, g_ref[...]).astype(_BF16)
    for c in range(N_FF_CHUNKS):
        cols = slice(c * FF_CHUNK, (c + 1) * FF_CHUNK)
        gate = _dot(h, wg_ref[:, cols])
        up = _dot(h, wu_ref[:, cols])
        a = (gate * _sigmoid(gate) * up).astype(_BF16)
        d = _dot(a, wd_ref[cols, :])
        if c == 0:
            acc_ref[...] = d
        else:
            acc_ref[...] += d
    y = x + 0.5 * acc_ref[...]
    if final_norm:
        y = _rmsnorm(y, gf_ref[...])
    o_ref[...] = y


def _const_spec(shape):
    zeros = (0,) * len(shape)
    return pl.BlockSpec(shape, lambda *_: zeros, pipeline_mode=pl.Buffered(1))


def _ffn(x2d, g, wg, wu, wd, gf, *, tm, final_norm):
    n = x2d.shape[0]
    assert n % tm == 0
    row_spec = pl.BlockSpec((tm, D_MODEL), lambda i: (i, 0))
    return pl.pallas_call(
        functools.partial(_ffn_kernel, final_norm=final_norm),
        grid=(n // tm,),
        in_specs=[row_spec, _const_spec(g.shape), _const_spec(wg.shape), _const_spec(wu.shape),
                  _const_spec(wd.shape), _const_spec(gf.shape)],
        out_specs=row_spec,
        out_shape=jax.ShapeDtypeStruct((n, D_MODEL), _F32),
        scratch_shapes=[pltpu.VMEM((tm, D_MODEL), _F32)],
        compiler_params=pltpu.CompilerParams(dimension_semantics=("arbitrary",),
                                             vmem_limit_bytes=VMEM_LIMIT),
        name="ffn_final" if final_norm else "ffn",
    )(x2d, g, wg, wu, wd, gf)


def _safe_scores(q, k, b, blk):
    gs = q.shape[0]
    ri = lax.broadcasted_iota(jnp.int32, (gs, gs), 0)
    ci = lax.broadcasted_iota(jnp.int32, (gs, gs), 1)
    qk = lax.dot_general(q.astype(_BF16), k.astype(_BF16), _NT, preferred_element_type=_F32)
    total = jnp.where(ri == ci, qk, 0.0)
    s = 1
    while s < blk:
        run = ri - ri % (2 * s)
        b_ref_rows = _dot_exact_rhs((ci == run + (s - 1)).astype(_BF16), b)
        qs = (q * jnp.exp(jnp.minimum(b - b_ref_rows, 0.0))).astype(_BF16)
        ks = (k * jnp.exp(jnp.minimum(b_ref_rows - b, 0.0))).astype(_BF16)
        level = (ri % (2 * s) >= s) & (ci % (2 * s) < s) & (run == ci - ci % (2 * s))
        total = total + jnp.where(level, lax.dot_general(qs, ks, _NT, preferred_element_type=_F32), 0.0)
        s *= 2
    return total


def _mixer_tile(pcur, st_from, st_to, x_ref, o_ref, mix_ref, gup_ref, gbias_ref, lb_ref, gon_ref, hon_ref, wout_ref,
                *, r0, tm, blk, fill, safe, st_keep=None):
    nb = tm // blk
    rows = slice(r0, r0 + tm)
    row = lax.broadcasted_iota(jnp.int32, (tm, tm), 0)
    col = lax.broadcasted_iota(jnp.int32, (tm, tm), 1)
    tri = ((col <= row) & (col >= row - row % blk)).astype(_BF16)
    lane = lax.broadcasted_iota(jnp.int32, (tm, LANES), 1)
    lb = lb_ref[...]

    def cat(x, y, axis):
        return jnp.concatenate([x, y], axis=axis)

    def cat_rows(parts):
        return parts[0] if len(parts) == 1 else jnp.concatenate(parts, axis=0)

    def bdiag(x, y):
        return cat(cat(x, jnp.zeros_like(y), 1), cat(jnp.zeros_like(x), y, 1), 0)

    gd = pcur(_GD, _GD + LANES).astype(_BF16)
    logits = jnp.dot(gd, gup_ref[...], preferred_element_type=_F32) + gbias_ref[...]
    g_cols = [(jnp.minimum(logits, 0.0) - jnp.log1p(jnp.exp(-jnp.abs(logits)))) / GLA_GATE_NORM]
    k_in = []
    for hd in range(HG_HEADS):
        z = pcur(_HF + hd * LANES, _HF + (hd + 1) * LANES)
        lb_h = lb[:, hd * LANES:(hd + 1) * LANES]
        e = jnp.exp(-jnp.abs(z))
        r = 1.0 / (1.0 + e)
        sig_pos = jnp.where(z >= 0, r, e * r)
        sig_neg = jnp.where(z >= 0, e * r, r)
        g_cols.append(jnp.log(lb_h + (1.0 - lb_h) * sig_pos))
        k_in.append((1.0 - lb_h) * sig_neg)
    fill(0)
    b_all = _dot_exact_rhs(tri, jnp.concatenate(g_cols, axis=1))
    steep = jnp.min(cat_rows([b_all[(c + 1) * blk - 1:(c + 1) * blk, :] for c in range(nb)])) < -DECAY_GUARD

    def k_scale(b):
        if not safe:
            return jnp.exp(-b)
        b_last = [jnp.broadcast_to(b[(c + 1) * blk - 1:(c + 1) * blk, :], (blk, LANES)) for c in range(nb)]
        return jnp.exp(cat_rows(b_last) - b)

    qs, ks, bs, qts, kts, vs, ebs = [], [], [], [], [], [], []
    for p in range(GLA_HEADS // 2):
        b = b_all[:, p * LANES:(p + 1) * LANES]
        eb = jnp.exp(b)
        ek = k_scale(b)
        q = pcur(_GQ + p * LANES, _GQ + (p + 1) * LANES) * (GLA_DK ** -0.5)
        k = pcur(_GK + p * LANES, _GK + (p + 1) * LANES)
        for hh in range(2):
            idx = 2 * p + hh
            m = (lane >= hh * GLA_DK) & (lane < (hh + 1) * GLA_DK)
            qs.append(jnp.where(m, q, 0.0))
            ks.append(jnp.where(m, k, 0.0))
            bs.append(b)
            qts.append((qs[-1] * eb).astype(_BF16))
            kts.append((ks[-1] * ek).astype(_BF16))
            vs.append(pcur(_GV + idx * GLA_DV, _GV + (idx + 1) * GLA_DV).astype(_BF16))
            ebs.append(eb)
    for hd in range(HG_HEADS):
        b = b_all[:, (2 + hd) * LANES:(3 + hd) * LANES]
        eb = jnp.exp(b)
        hq = pcur(_HQ + hd * LANES, _HQ + (hd + 1) * LANES)
        qs.append(hq * _sigmoid(hq))
        ks.append(k_in[hd])
        bs.append(b)
        qts.append((qs[-1] * eb).astype(_BF16))
        kts.append((ks[-1] * k_scale(b)).astype(_BF16))
        vs.append(pcur(_HI + hd * LANES, _HI + (hd + 1) * LANES).astype(_BF16))
        ebs.append(eb)

    gs = min(tm, LANES)
    groups = [slice(u * gs, (u + 1) * gs) for u in range(tm // gs)]
    blocks = [slice(c * blk, (c + 1) * blk) for c in range(nb)]
    pairs = [(2 * p, 2 * p + 1) for p in range(N_HEADS // 2)]
    row2 = lax.broadcasted_iota(jnp.int32, (gs, 2 * gs), 0)
    col2 = lax.broadcasted_iota(jnp.int32, (gs, 2 * gs), 1)
    col2 = jnp.where(col2 >= gs, col2 - gs, col2)
    causal2 = (col2 <= row2) & (col2 >= row2 - row2 % blk)

    q2 = [cat(qts[a], qts[b], 1) for a, b in pairs]
    fill(1)
    kvs = [[lax.dot_general(cat(vs[a][rs], vs[b][rs], 0), bdiag(kts[a][rs], kts[b][rs]), _TN,
                            preferred_element_type=_F32)
            for rs in blocks] for a, b in pairs]

    if safe:
        scores = [[cat(_safe_scores(qs[a][g], ks[a][g], bs[a][g], blk),
                       _safe_scores(qs[b][g], ks[b][g], bs[b][g], blk), 1) for g in groups] for a, b in pairs]
    else:
        scores = [[lax.dot_general(q2[p][g], bdiag(kts[a][g], kts[b][g]), _NT, preferred_element_type=_F32)
                   for g in groups] for p, (a, b) in enumerate(pairs)]
    fill(2)
    o_intra = [[jnp.dot(jnp.where(causal2, scores[p][u], 0.0).astype(_BF16), bdiag(vs[a][g], vs[b][g]),
                        preferred_element_type=_F32)
                for u, g in enumerate(groups)] for p, (a, b) in enumerate(pairs)]
    fill(3)
    states = [cat(st_from[a], st_from[b], 1) for a, b in pairs]
    if st_keep is not None:
        for h in range(N_HEADS):
            st_keep[h] = states[h // 2][:, (h % 2) * LANES:(h % 2 + 1) * LANES]
    e2 = [cat(ebs[a], ebs[b], 1) for a, b in pairs]
    o_inter = [[] for _ in pairs]
    for c, rs in enumerate(blocks):
        for p in range(len(pairs)):
            s2 = states[p].astype(_BF16)
            o_inter[p].append(lax.dot_general(q2[p][rs], bdiag(s2[:, :LANES], s2[:, LANES:]), _NT,
                                              preferred_element_type=_F32))
        for p in range(len(pairs)):
            e_last = e2[p][(c + 1) * blk - 1:(c + 1) * blk, :]
            states[p] = states[p] * e_last + kvs[p][c] if safe else (states[p] + kvs[p][c]) * e_last
    fill(4)
    for p, (a, b) in enumerate(pairs):
        st_to[a] = states[p][:, :LANES]
        st_to[b] = states[p][:, LANES:]
        o2 = cat_rows(o_intra[p]) + cat_rows(o_inter[p])
        for hh, h in enumerate((a, b)):
            o = o2[:, hh * LANES:(hh + 1) * LANES]
            onorm = gon_ref[...] if h < GLA_HEADS else hon_ref[...]
            gate_col = _GR + h * LANES if h < GLA_HEADS else _HGATE + (h - GLA_HEADS) * LANES
            gate = pcur(gate_col, gate_col + LANES)
            o = o * lax.rsqrt(jnp.mean(o * o, axis=-1, keepdims=True) + EPS) * onorm
            o = o * (gate * _sigmoid(gate))
            mix_ref[rows, h * LANES:(h + 1) * LANES] = o.astype(_BF16)
    fill(5)
    out = jnp.dot(mix_ref[rows, :], wout_ref[...], preferred_element_type=_F32)
    fill(6)
    o_ref[rows, :] = x_ref[rows, :] + out
    return steep


def _mixer_kernel(x_ref, xn_ref, st_in_ref, g_ref, win_ref, gup_ref, gbias_ref, lb_ref, gon_ref, hon_ref, wout_ref,
                  o_ref, st_ref, pa_ref, pb_ref, ha_ref, hb_ref, mix_ref, s0_ref, s1_ref, *, tm, blk, pipelined):
    u = pl.program_id(1)
    tile = functools.partial(_mixer_tile, x_ref=x_ref, o_ref=o_ref, mix_ref=mix_ref, gup_ref=gup_ref,
                             gbias_ref=gbias_ref, lb_ref=lb_ref, gon_ref=gon_ref, hon_ref=hon_ref,
                             wout_ref=wout_ref, tm=tm, blk=blk)
    no_fill = lambda stage: None

    def normed(src_ref, r0):
        return _rmsnorm(src_ref[r0:r0 + tm, :], g_ref[...]).astype(_BF16)

    valid = {id(pa_ref): set(range(N_PROJ_PIECES)), id(pb_ref): set()}

    def project(h, dst_ref, pieces, for_reader):
        for i in pieces:
            cols = slice(i * PROJ_PIECE, min((i + 1) * PROJ_PIECE, D_PROJ))
            dst_ref[:, cols] = jnp.dot(h, win_ref[:, cols], preferred_element_type=_F32)
            (valid[id(dst_ref)].add if for_reader else valid[id(dst_ref)].discard)(i)

    def reader(buf_ref):
        def read(c0, c1):
            assert all(i in valid[id(buf_ref)] for i in range(c0 // PROJ_PIECE, (c1 - 1) // PROJ_PIECE + 1))
            return buf_ref[:, c0:c1]
        return read

    def filler(h_ref, dst_ref, for_reader):
        pieces = iter(range(N_PROJ_PIECES))
        return lambda stage: project(h_ref[...], dst_ref, [next(pieces) for _ in range(PROJ_SCHEDULE[stage])],
                                     for_reader)

    @pl.when(u == 0)
    def _():
        for h in range(N_HEADS):
            st_ref[h] = st_in_ref[h].T

    if pipelined:
        @pl.when((pl.program_id(0) == 0) & (u == 0))
        def _():
            project(normed(x_ref, 0), pa_ref, range(N_PROJ_PIECES), True)
            hb_ref[...] = normed(x_ref, tm)

        ha_ref[...] = normed(xn_ref, 0)
        steep_a = tile(reader(pa_ref), st_ref, s1_ref, r0=0, fill=filler(hb_ref, pb_ref, True), safe=False,
                       st_keep=s0_ref)
        steep_b = tile(reader(pb_ref), s1_ref, st_ref, r0=tm, fill=filler(ha_ref, pa_ref, False), safe=False)
        hb_ref[...] = normed(xn_ref, tm)

        @pl.when(steep_a)
        def _():
            project(normed(x_ref, 0), pb_ref, range(N_PROJ_PIECES), True)
            tile(reader(pb_ref), s0_ref, s1_ref, r0=0, fill=no_fill, safe=True)
            project(normed(x_ref, tm), pb_ref, range(N_PROJ_PIECES), True)

        @pl.when(steep_a | steep_b)
        def _():
            tile(reader(pb_ref), s1_ref, st_ref, r0=tm, fill=no_fill, safe=True)
    else:
        project(normed(x_ref, 0), pa_ref, range(N_PROJ_PIECES), True)
        steep_a = tile(reader(pa_ref), st_ref, st_ref, r0=0, fill=no_fill, safe=False, st_keep=s0_ref)

        @pl.when(steep_a)
        def _():
            tile(reader(pa_ref), s0_ref, st_ref, r0=0, fill=no_fill, safe=True)


    @pl.when(u == pl.num_programs(1) - 1)
    def _():
        for h in range(N_HEADS):
            st_ref[h] = st_ref[h].T


def _mixer(x, st_in, g, win, gup, gbias, lb, gon, hon, wout, *, tm, blk):
    bsz, seq, _ = x.shape
    pipelined = seq % (2 * tm) == 0
    rows = 2 * tm if pipelined else tm
    assert seq % rows == 0 and tm % blk == 0 and blk & (blk - 1) == 0
    nu = seq // rows
    last = bsz * nu - 1

    def next_block(b, u):
        nxt = jnp.minimum(b * nu + u + 1, last)
        return (nxt // nu, nxt % nu, 0)

    row_spec = pl.BlockSpec((None, rows, D_MODEL), lambda b, u: (b, u, 0))
    st_spec = pl.BlockSpec((None, N_HEADS, LANES, LANES), lambda b, u: (b, 0, 0, 0))
    return pl.pallas_call(
        functools.partial(_mixer_kernel, tm=tm, blk=blk, pipelined=pipelined),
        grid=(bsz, nu),
        in_specs=[row_spec, pl.BlockSpec((None, rows, D_MODEL), next_block), st_spec]
        + [_const_spec(a.shape) for a in (g, win, gup, gbias, lb, gon, hon, wout)],
        out_specs=[row_spec, st_spec],
        out_shape=[jax.ShapeDtypeStruct(x.shape, _F32),
                   jax.ShapeDtypeStruct((bsz, N_HEADS, LANES, LANES), _F32)],
        scratch_shapes=[pltpu.VMEM((tm, D_PROJ), _F32), pltpu.VMEM((tm, D_PROJ), _F32),
                        pltpu.VMEM((tm, D_MODEL), _BF16), pltpu.VMEM((tm, D_MODEL), _BF16),
                        pltpu.VMEM((rows, D_MODEL), _BF16),
                        pltpu.VMEM((N_HEADS, LANES, LANES), _F32), pltpu.VMEM((N_HEADS, LANES, LANES), _F32)],
        compiler_params=pltpu.CompilerParams(dimension_semantics=("arbitrary", "arbitrary"),
                                             vmem_limit_bytes=VMEM_LIMIT),
        name="mixer",
    )(x, x, st_in, g, win, gup, gbias, lb, gon, hon, wout)


def _pack_state(s_gla, s_hg):
    zeros = jnp.zeros_like(s_gla)
    even = jnp.concatenate([s_gla, zeros], axis=2)
    odd = jnp.concatenate([zeros, s_gla], axis=2)
    is_odd = (jnp.arange(GLA_HEADS) % 2 == 1)[None, :, None, None]
    gla = jnp.where(is_odd, odd, even)
    return jnp.concatenate([gla, s_hg], axis=1).astype(_F32)


def _unpack_state(st):
    gla = st[:, :GLA_HEADS]
    is_odd = (jnp.arange(GLA_HEADS) % 2 == 1)[None, :, None, None]
    gla = jnp.where(is_odd, gla[:, :, GLA_DK:], gla[:, :, :GLA_DK])
    return gla, st[:, GLA_HEADS:]


def kernel(x_prompt, x_sample, state_gla, state_hgrn, norm_ffn1, ffn1_w_gate, ffn1_w_up, ffn1_w_down, norm_mix, w_in, gla_gate_up, gla_gate_bias, gla_onorm, hg_lower_bound_logits, hg_onorm, w_out, norm_ffn2, ffn2_w_gate, ffn2_w_up, ffn2_w_down, norm_final):
    assert norm_ffn1.shape[0] == 1, "single layer"
    row = lambda v: v.reshape(1, -1).astype(_F32)
    bf = lambda w: w.astype(_BF16)
    f1 = (row(norm_ffn1[0]), bf(ffn1_w_gate[0]), bf(ffn1_w_up[0]), bf(ffn1_w_down[0]), row(norm_final))
    f2 = (row(norm_ffn2[0]), bf(ffn2_w_gate[0]), bf(ffn2_w_up[0]), bf(ffn2_w_down[0]), row(norm_final))
    sizes = (GLA_HEADS * GLA_DK, GLA_HEADS * GLA_DK, GLA_HEADS * GLA_DV, GLA_HEADS * GLA_DV, GLA_GATE_RANK,
             HG_HEADS * HG_EXPAND, HG_HEADS * HG_EXPAND, HG_HEADS * HG_HEAD_DIM, HG_HEADS * HG_HEAD_DIM)
    splits = [sum(sizes[:i + 1]) for i in range(len(sizes) - 1)]
    gq, gk, gv, gr, gd, hq, hf, hi, hgate = jnp.split(w_in[0], splits, axis=1)
    gd = jnp.concatenate([gd, jnp.zeros((D_MODEL, _HF - GLA_GATE_RANK), gd.dtype)], axis=1)
    win = jnp.concatenate([gd, hf, gq, gk, hq, gv, hi, gr, hgate], axis=1).astype(_BF16)
    gup = jnp.concatenate([gla_gate_up[0], jnp.zeros((LANES - GLA_GATE_RANK, GLA_HEADS * GLA_DK), _F32)],
                          axis=0).astype(_BF16)
    lb = jnp.cumsum(jax.nn.softmax(hg_lower_bound_logits.astype(_F32), axis=0), axis=0)[0]
    mix_w = (row(norm_mix[0]), win, gup, row(gla_gate_bias[0]), row(lb), row(gla_onorm[0]), row(hg_onorm[0]),
             bf(w_out[0]))

    def trunk(x, s_gla, s_hg, tm_ffn, tm_mix, blk):
        bsz, seq, _ = x.shape
        x1 = _ffn(x.reshape(bsz * seq, D_MODEL), *f1, tm=tm_ffn, final_norm=False)
        x2, st = _mixer(x1.reshape(bsz, seq, D_MODEL), _pack_state(s_gla, s_hg), *mix_w, tm=tm_mix, blk=blk)
        y = _ffn(x2.reshape(bsz * seq, D_MODEL), *f2, tm=tm_ffn, final_norm=True)
        g, h = _unpack_state(st)
        return y.reshape(bsz, seq, D_MODEL), g[None], h[None]

    bp, sp, _ = x_prompt.shape
    bs, ss, _ = x_sample.shape
    zg = jnp.zeros((bp, GLA_HEADS, GLA_DK, GLA_DV), _F32)
    zh = jnp.zeros((bp, HG_HEADS, HG_EXPAND, HG_HEAD_DIM), _F32)
    blk_p = min(CHUNK, sp)
    y_p, g_p, h_p = trunk(x_prompt, zg, zh, FFN_ROWS, min(MIX_ROWS, sp), blk_p)
    y_s, g_s, h_s = trunk(x_sample, state_gla[0], state_hgrn[0], bs * ss, ss, ss)
    sd_g, sd_h = state_gla.dtype, state_hgrn.dtype
    return (y_p, y_s, g_p.astype(sd_g), h_p.astype(sd_h), g_s.astype(sd_g), h_s.astype(sd_h))
```

```python
import functools

import jax
import jax.numpy as jnp
from jax import lax
from jax.experimental import pallas as pl
from jax.experimental.pallas import tpu as pltpu

D_MODEL = 1024
CHUNK = 64
GLA_HEADS = 4
GLA_DK = 64
GLA_DV = 128
GLA_GATE_RANK = 16
GLA_GATE_NORM = 16.0
HG_HEADS = 4
HG_EXPAND = 128
HG_HEAD_DIM = 128
D_FF = 2816
EPS = 1e-6

LANES = 128
MXU_COLS = 256
N_HEADS = GLA_HEADS + HG_HEADS
FF_CHUNK = MXU_COLS
N_FF_CHUNKS = D_FF // FF_CHUNK
FFN_ROWS = 512
MIX_ROWS = 256
VMEM_LIMIT = 56 * 1024 * 1024
DECAY_GUARD = 60.0

_GD, _HF, _GQ, _GK, _HQ, _GV, _HI, _GR, _HGATE = 0, 256, 768, 1024, 1280, 1792, 2304, 2816, 3328
D_PROJ = _HGATE + HG_HEADS * HG_HEAD_DIM
PROJ_PIECE = 2 * MXU_COLS
N_PROJ_PIECES = -(-D_PROJ // PROJ_PIECE)
PROJ_SCHEDULE = (2, 2, 1, 1, 0, 1, 1)
assert sum(PROJ_SCHEDULE) == N_PROJ_PIECES

_F32 = jnp.float32
_BF16 = jnp.bfloat16


def _rmsnorm(x, g):
    return x * lax.rsqrt(jnp.mean(x * x, axis=-1, keepdims=True) + EPS) * g


def _sigmoid(x):
    return 1.0 / (1.0 + jnp.exp(-x))


def _dot(a, b):
    return jnp.dot(a, b, preferred_element_type=_F32)


_NT = (((1,), (1,)), ((), ()))
_TN = (((0,), (0,)), ((), ()))


def _dot_exact_rhs(sel, x):
    hi = x.astype(_BF16)
    r1 = x - hi.astype(_F32)
    mid = r1.astype(_BF16)
    lo = (r1 - mid.astype(_F32)).astype(_BF16)
    n = x.shape[1]
    parts = jnp.concatenate([hi, mid, lo], axis=1)
    r = jnp.dot(sel, parts, preferred_element_type=_F32)
    return r[:, :n] + r[:, n:2 * n] + r[:, 2 * n:]


def _ffn_kernel(x_ref, g_ref, wg_ref, wu_ref, wd_ref, gf_ref, o_ref, acc_ref, *, final_norm):
    x = x_ref[...]
    h = _rmsnorm(x, g_ref[...]).astype(_BF16)
    for c in range(N_FF_CHUNKS):
        cols = slice(c * FF_CHUNK, (c + 1) * FF_CHUNK)
        gate = _dot(h, wg_ref[:, cols])
        up = _dot(h, wu_ref[:, cols])
        a = (gate * _sigmoid(gate) * up).astype(_BF16)
        d = _dot(a, wd_ref[cols, :])
        if c == 0:
            acc_ref[...] = d
        else:
            acc_ref[...] += d
    y = x + 0.5 * acc_ref[...]
    if final_norm:
        y = _rmsnorm(y, gf_ref[...])
    o_ref[...] = y


def _const_spec(shape):
    zeros = (0,) * len(shape)
    return pl.BlockSpec(shape, lambda *_: zeros, pipeline_mode=pl.Buffered(1))


def _ffn(x2d, g, wg, wu, wd, gf, *, tm, final_norm):
    n = x2d.shape[0]
    assert n % tm == 0
    row_spec = pl.BlockSpec((tm, D_MODEL), lambda i: (i, 0))
    return pl.pallas_call(
        functools.partial(_ffn_kernel, final_norm=final_norm),
        grid=(n // tm,),
        in_specs=[row_spec, _const_spec(g.shape), _const_spec(wg.shape), _const_spec(wu.shape),
                  _const_spec(wd.shape), _const_spec(gf.shape)],
        out_specs=row_spec,
        out_shape=jax.ShapeDtypeStruct((n, D_MODEL), _F32),
        scratch_shapes=[pltpu.VMEM((tm, D_MODEL), _F32)],
        compiler_params=pltpu.CompilerParams(dimension_semantics=("arbitrary",),
                                             vmem_limit_bytes=VMEM_LIMIT),
        name="ffn_final" if final_norm else "ffn",
    )(x2d, g, wg, wu, wd, gf)


def _safe_scores(q, k, b, blk):
    gs = q.shape[0]
    ri = lax.broadcasted_iota(jnp.int32, (gs, gs), 0)
    ci = lax.broadcasted_iota(jnp.int32, (gs, gs), 1)
    qk = lax.dot_general(q.astype(_BF16), k.astype(_BF16), _NT, preferred_element_type=_F32)
    total = jnp.where(ri == ci, qk, 0.0)
    s = 1
    while s < blk:
        run = ri - ri % (2 * s)
        b_ref_rows = _dot_exact_rhs((ci == run + (s - 1)).astype(_BF16), b)
        qs = (q * jnp.exp(jnp.minimum(b - b_ref_rows, 0.0))).astype(_BF16)
        ks = (k * jnp.exp(jnp.minimum(b_ref_rows - b, 0.0))).astype(_BF16)
        level = (ri % (2 * s) >= s) & (ci % (2 * s) < s) & (run == ci - ci % (2 * s))
        total = total + jnp.where(level, lax.dot_general(qs, ks, _NT, preferred_element_type=_F32), 0.0)
        s *= 2
    return total


def _mixer_tile(pcur, st_from, st_to, x_ref, o_ref, mix_ref, gup_ref, gbias_ref, lb_ref, gon_ref, hon_ref, wout_ref,
                *, r0, tm, blk, fill, safe, st_keep=None):
    nb = tm // blk
    rows = slice(r0, r0 + tm)
    row = lax.broadcasted_iota(jnp.int32, (tm, tm), 0)
    col = lax.broadcasted_iota(jnp.int32, (tm, tm), 1)
    tri = ((col <= row) & (col >= row - row % blk)).astype(_BF16)
    lane = lax.broadcasted_iota(jnp.int32, (tm, LANES), 1)
    lb = lb_ref[...]

    def cat(x, y, axis):
        return jnp.concatenate([x, y], axis=axis)

    def cat_rows(parts):
        return parts[0] if len(parts) == 1 else jnp.concatenate(parts, axis=0)

    def bdiag(x, y):
        return cat(cat(x, jnp.zeros_like(y), 1), cat(jnp.zeros_like(x), y, 1), 0)

    gd = pcur(_GD, _GD + LANES).astype(_BF16)
    logits = jnp.dot(gd, gup_ref[...], preferred_element_type=_F32) + gbias_ref[...]
    g_cols = [(jnp.minimum(logits, 0.0) - jnp.log1p(jnp.exp(-jnp.abs(logits)))) / GLA_GATE_NORM]
    k_in = []
    for hd in range(HG_HEADS):
        z = pcur(_HF + hd * LANES, _HF + (hd + 1) * LANES)
        lb_h = lb[:, hd * LANES:(hd + 1) * LANES]
        e = jnp.exp(-jnp.abs(z))
        r = 1.0 / (1.0 + e)
        sig_pos = jnp.where(z >= 0, r, e * r)
        sig_neg = jnp.where(z >= 0, e * r, r)
        g_cols.append(jnp.log(lb_h + (1.0 - lb_h) * sig_pos))
        k_in.append((1.0 - lb_h) * sig_neg)
    fill(0)
    b_all = _dot_exact_rhs(tri, jnp.concatenate(g_cols, axis=1))
    b_min = b_all[blk - 1:blk, :]
    for c in range(1, nb):
        b_min = jnp.minimum(b_min, b_all[(c + 1) * blk - 1:(c + 1) * blk, :])
    steep = jnp.min(jnp.broadcast_to(b_min, (8, b_min.shape[1]))) < -DECAY_GUARD

    def k_scale(b):
        if not safe:
            return jnp.exp(-b)
        b_last = [jnp.broadcast_to(b[(c + 1) * blk - 1:(c + 1) * blk, :], (blk, LANES)) for c in range(nb)]
        return jnp.exp(cat_rows(b_last) - b)

    qs, ks, bs, qts, kts, vs, ebs = [], [], [], [], [], [], []
    for p in range(GLA_HEADS // 2):
        b = b_all[:, p * LANES:(p + 1) * LANES]
        eb = jnp.exp(b)
        ek = k_scale(b)
        q = pcur(_GQ + p * LANES, _GQ + (p + 1) * LANES) * (GLA_DK ** -0.5)
        k = pcur(_GK + p * LANES, _GK + (p + 1) * LANES)
        for hh in range(2):
            idx = 2 * p + hh
            m = (lane >= hh * GLA_DK) & (lane < (hh + 1) * GLA_DK)
            qs.append(jnp.where(m, q, 0.0))
            ks.append(jnp.where(m, k, 0.0))
            bs.append(b)
            qts.append((qs[-1] * eb).astype(_BF16))
            kts.append((ks[-1] * ek).astype(_BF16))
            vs.append(pcur(_GV + idx * GLA_DV, _GV + (idx + 1) * GLA_DV).astype(_BF16))
            ebs.append(eb)
    for hd in range(HG_HEADS):
        b = b_all[:, (2 + hd) * LANES:(3 + hd) * LANES]
        eb = jnp.exp(b)
        hq = pcur(_HQ + hd * LANES, _HQ + (hd + 1) * LANES)
        qs.append(hq * _sigmoid(hq))
        ks.append(k_in[hd])
        bs.append(b)
        qts.append((qs[-1] * eb).astype(_BF16))
        kts.append((ks[-1] * k_scale(b)).astype(_BF16))
        vs.append(pcur(_HI + hd * LANES, _HI + (hd + 1) * LANES).astype(_BF16))
        ebs.append(eb)

    gs = min(tm, LANES)
    groups = [slice(u * gs, (u + 1) * gs) for u in range(tm // gs)]
    blocks = [slice(c * blk, (c + 1) * blk) for c in range(nb)]
    pairs = [(2 * p, 2 * p + 1) for p in range(N_HEADS // 2)]
    row2 = lax.broadcasted_iota(jnp.int32, (gs, 2 * gs), 0)
    col2 = lax.broadcasted_iota(jnp.int32, (gs, 2 * gs), 1)
    col2 = jnp.where(col2 >= gs, col2 - gs, col2)
    causal2 = (col2 <= row2) & (col2 >= row2 - row2 % blk)

    q2 = [cat(qts[a], qts[b], 1) for a, b in pairs]
    fill(1)
    kvs = [[lax.dot_general(cat(vs[a][rs], vs[b][rs], 0), bdiag(kts[a][rs], kts[b][rs]), _TN,
                            preferred_element_type=_F32)
            for rs in blocks] for a, b in pairs]

    if safe:
        scores = [[cat(_safe_scores(qs[a][g], ks[a][g], bs[a][g], blk),
                       _safe_scores(qs[b][g], ks[b][g], bs[b][g], blk), 1) for g in groups] for a, b in pairs]
    else:
        scores = [[lax.dot_general(q2[p][g], bdiag(kts[a][g], kts[b][g]), _NT, preferred_element_type=_F32)
                   for g in groups] for p, (a, b) in enumerate(pairs)]
    fill(2)
    o_intra = [[jnp.dot(jnp.where(causal2, scores[p][u], 0.0).astype(_BF16), bdiag(vs[a][g], vs[b][g]),
                        preferred_element_type=_F32)
                for u, g in enumerate(groups)] for p, (a, b) in enumerate(pairs)]
    fill(3)
    states = [cat(st_from[a], st_from[b], 1) for a, b in pairs]
    if st_keep is not None:
        for h in range(N_HEADS):
            st_keep[h] = states[h // 2][:, (h % 2) * LANES:(h % 2 + 1) * LANES]
    e2 = [cat(ebs[a], ebs[b], 1) for a, b in pairs]
    o_inter = [[] for _ in pairs]
    for c, rs in enumerate(blocks):
        for p in range(len(pairs)):
            s2 = states[p].astype(_BF16)
            o_inter[p].append(lax.dot_general(q2[p][rs], bdiag(s2[:, :LANES], s2[:, LANES:]), _NT,
                                              preferred_element_type=_F32))
        for p in range(len(pairs)):
            e_last = e2[p][(c + 1) * blk - 1:(c + 1) * blk, :]
            states[p] = states[p] * e_last + kvs[p][c] if safe else (states[p] + kvs[p][c]) * e_last
    fill(4)
    for p, (a, b) in enumerate(pairs):
        st_to[a] = states[p][:, :LANES]
        st_to[b] = states[p][:, LANES:]
        o2 = cat_rows(o_intra[p]) + cat_rows(o_inter[p])
        for hh, h in enumerate((a, b)):
            o = o2[:, hh * LANES:(hh + 1) * LANES]
            onorm = gon_ref[...] if h < GLA_HEADS else hon_ref[...]
            gate_col = _GR + h * LANES if h < GLA_HEADS else _HGATE + (h - GLA_HEADS) * LANES
            gate = pcur(gate_col, gate_col + LANES)
            o = o * lax.rsqrt(jnp.mean(o * o, axis=-1, keepdims=True) + EPS) * onorm
            o = o * (gate * _sigmoid(gate))
            mix_ref[rows, h * LANES:(h + 1) * LANES] = o.astype(_BF16)
    fill(5)
    out = jnp.dot(mix_ref[rows, :], wout_ref[...], preferred_element_type=_F32)
    fill(6)
    o_ref[rows, :] = x_ref[rows, :] + out
    return steep


def _mixer_kernel(x_ref, xn_ref, st_in_ref, g_ref, win_ref, gup_ref, gbias_ref, lb_ref, gon_ref, hon_ref, wout_ref,
                  o_ref, st_ref, pa_ref, pb_ref, ha_ref, hb_ref, mix_ref, s0_ref, s1_ref, *, tm, blk, pipelined):
    u = pl.program_id(1)
    tile = functools.partial(_mixer_tile, x_ref=x_ref, o_ref=o_ref, mix_ref=mix_ref, gup_ref=gup_ref,
                             gbias_ref=gbias_ref, lb_ref=lb_ref, gon_ref=gon_ref, hon_ref=hon_ref,
                             wout_ref=wout_ref, tm=tm, blk=blk)
    no_fill = lambda stage: None

    def normed(src_ref, r0):
        return _rmsnorm(src_ref[r0:r0 + tm, :], g_ref[...]).astype(_BF16)

    valid = {id(pa_ref): set(range(N_PROJ_PIECES)), id(pb_ref): set()}

    def project(h, dst_ref, pieces, for_reader):
        for i in pieces:
            cols = slice(i * PROJ_PIECE, min((i + 1) * PROJ_PIECE, D_PROJ))
            dst_ref[:, cols] = jnp.dot(h, win_ref[:, cols], preferred_element_type=_F32)
            (valid[id(dst_ref)].add if for_reader else valid[id(dst_ref)].discard)(i)

    def reader(buf_ref):
        def read(c0, c1):
            assert all(i in valid[id(buf_ref)] for i in range(c0 // PROJ_PIECE, (c1 - 1) // PROJ_PIECE + 1))
            return buf_ref[:, c0:c1]
        return read

    def filler(h_ref, dst_ref, for_reader):
        pieces = iter(range(N_PROJ_PIECES))
        return lambda stage: project(h_ref[...], dst_ref, [next(pieces) for _ in range(PROJ_SCHEDULE[stage])],
                                     for_reader)

    @pl.when(u == 0)
    def _():
        for h in range(N_HEADS):
            st_ref[h] = st_in_ref[h].T

    if pipelined:
        @pl.when((pl.program_id(0) == 0) & (u == 0))
        def _():
            project(normed(x_ref, 0), pa_ref, range(N_PROJ_PIECES), True)
            hb_ref[...] = normed(x_ref, tm)

        ha_ref[...] = normed(xn_ref, 0)
        steep_a = tile(reader(pa_ref), st_ref, s1_ref, r0=0, fill=filler(hb_ref, pb_ref, True), safe=False,
                       st_keep=s0_ref)
        steep_b = tile(reader(pb_ref), s1_ref, st_ref, r0=tm, fill=filler(ha_ref, pa_ref, False), safe=False)
        hb_ref[...] = normed(xn_ref, tm)

        @pl.when(steep_a)
        def _():
            project(normed(x_ref, 0), pb_ref, range(N_PROJ_PIECES), True)
            tile(reader(pb_ref), s0_ref, s1_ref, r0=0, fill=no_fill, safe=True)
            project(normed(x_ref, tm), pb_ref, range(N_PROJ_PIECES), True)

        @pl.when(steep_a | steep_b)
        def _():
            tile(reader(pb_ref), s1_ref, st_ref, r0=tm, fill=no_fill, safe=True)
    else:
        project(normed(x_ref, 0), pa_ref, range(N_PROJ_PIECES), True)
        steep_a = tile(reader(pa_ref), st_ref, st_ref, r0=0, fill=no_fill, safe=False, st_keep=s0_ref)

        @pl.when(steep_a)
        def _():
            tile(reader(pa_ref), s0_ref, st_ref, r0=0, fill=no_fill, safe=True)


    @pl.when(u == pl.num_programs(1) - 1)
    def _():
        for h in range(N_HEADS):
            st_ref[h] = st_ref[h].T


def _mixer(x, st_in, g, win, gup, gbias, lb, gon, hon, wout, *, tm, blk):
    bsz, seq, _ = x.shape
    pipelined = seq % (2 * tm) == 0
    rows = 2 * tm if pipelined else tm
    assert seq % rows == 0 and tm % blk == 0 and blk & (blk - 1) == 0
    nu = seq // rows
    last = bsz * nu - 1

    def next_block(b, u):
        nxt = jnp.minimum(b * nu + u + 1, last)
        return (nxt // nu, nxt % nu, 0)

    row_spec = pl.BlockSpec((None, rows, D_MODEL), lambda b, u: (b, u, 0))
    st_spec = pl.BlockSpec((None, N_HEADS, LANES, LANES), lambda b, u: (b, 0, 0, 0))
    return pl.pallas_call(
        functools.partial(_mixer_kernel, tm=tm, blk=blk, pipelined=pipelined),
        grid=(bsz, nu),
        in_specs=[row_spec, pl.BlockSpec((None, rows, D_MODEL), next_block), st_spec]
        + [_const_spec(a.shape) for a in (g, win, gup, gbias, lb, gon, hon, wout)],
        out_specs=[row_spec, st_spec],
        out_shape=[jax.ShapeDtypeStruct(x.shape, _F32),
                   jax.ShapeDtypeStruct((bsz, N_HEADS, LANES, LANES), _F32)],
        scratch_shapes=[pltpu.VMEM((tm, D_PROJ), _F32), pltpu.VMEM((tm, D_PROJ), _F32),
                        pltpu.VMEM((tm, D_MODEL), _BF16), pltpu.VMEM((tm, D_MODEL), _BF16),
                        pltpu.VMEM((rows, D_MODEL), _BF16),
                        pltpu.VMEM((N_HEADS, LANES, LANES), _F32), pltpu.VMEM((N_HEADS, LANES, LANES), _F32)],
        compiler_params=pltpu.CompilerParams(dimension_semantics=("arbitrary", "arbitrary"),
                                             vmem_limit_bytes=VMEM_LIMIT),
        name="mixer",
    )(x, x, st_in, g, win, gup, gbias, lb, gon, hon, wout)


def _pack_state(s_gla, s_hg):
    zeros = jnp.zeros_like(s_gla)
    even = jnp.concatenate([s_gla, zeros], axis=2)
    odd = jnp.concatenate([zeros, s_gla], axis=2)
    is_odd = (jnp.arange(GLA_HEADS) % 2 == 1)[None, :, None, None]
    gla = jnp.where(is_odd, odd, even)
    return jnp.concatenate([gla, s_hg], axis=1).astype(_F32)


def _unpack_state(st):
    gla = st[:, :GLA_HEADS]
    is_odd = (jnp.arange(GLA_HEADS) % 2 == 1)[None, :, None, None]
    gla = jnp.where(is_odd, gla[:, :, GLA_DK:], gla[:, :, :GLA_DK])
    return gla, st[:, GLA_HEADS:]


def kernel(x_prompt, x_sample, state_gla, state_hgrn, norm_ffn1, ffn1_w_gate, ffn1_w_up, ffn1_w_down, norm_mix, w_in, gla_gate_up, gla_gate_bias, gla_onorm, hg_lower_bound_logits, hg_onorm, w_out, norm_ffn2, ffn2_w_gate, ffn2_w_up, ffn2_w_down, norm_final):
    assert norm_ffn1.shape[0] == 1, "single layer"
    row = lambda v: v.reshape(1, -1).astype(_F32)
    bf = lambda w: w.astype(_BF16)
    f1 = (row(norm_ffn1[0]), bf(ffn1_w_gate[0]), bf(ffn1_w_up[0]), bf(ffn1_w_down[0]), row(norm_final))
    f2 = (row(norm_ffn2[0]), bf(ffn2_w_gate[0]), bf(ffn2_w_up[0]), bf(ffn2_w_down[0]), row(norm_final))
    sizes = (GLA_HEADS * GLA_DK, GLA_HEADS * GLA_DK, GLA_HEADS * GLA_DV, GLA_HEADS * GLA_DV, GLA_GATE_RANK,
             HG_HEADS * HG_EXPAND, HG_HEADS * HG_EXPAND, HG_HEADS * HG_HEAD_DIM, HG_HEADS * HG_HEAD_DIM)
    splits = [sum(sizes[:i + 1]) for i in range(len(sizes) - 1)]
    gq, gk, gv, gr, gd, hq, hf, hi, hgate = jnp.split(w_in[0], splits, axis=1)
    gd = jnp.concatenate([gd, jnp.zeros((D_MODEL, _HF - GLA_GATE_RANK), gd.dtype)], axis=1)
    win = jnp.concatenate([gd, hf, gq, gk, hq, gv, hi, gr, hgate], axis=1).astype(_BF16)
    gup = jnp.concatenate([gla_gate_up[0], jnp.zeros((LANES - GLA_GATE_RANK, GLA_HEADS * GLA_DK), _F32)],
                          axis=0).astype(_BF16)
    lb = jnp.cumsum(jax.nn.softmax(hg_lower_bound_logits.astype(_F32), axis=0), axis=0)[0]
    mix_w = (row(norm_mix[0]), win, gup, row(gla_gate_bias[0]), row(lb), row(gla_onorm[0]), row(hg_onorm[0]),
             bf(w_out[0]))

    def trunk(x, s_gla, s_hg, tm_ffn, tm_mix, blk):
        bsz, seq, _ = x.shape
        x1 = _ffn(x.reshape(bsz * seq, D_MODEL), *f1, tm=tm_ffn, final_norm=False)
        x2, st = _mixer(x1.reshape(bsz, seq, D_MODEL), _pack_state(s_gla, s_hg), *mix_w, tm=tm_mix, blk=blk)
        y = _ffn(x2.reshape(bsz * seq, D_MODEL), *f2, tm=tm_ffn, final_norm=True)
        g, h = _unpack_state(st)
        return y.reshape(bsz, seq, D_MODEL), g[None], h[None]

    bp, sp, _ = x_prompt.shape
    bs, ss, _ = x_sample.shape
    zg = jnp.zeros((bp, GLA_HEADS, GLA_DK, GLA_DV), _F32)
    zh = jnp.zeros((bp, HG_HEADS, HG_EXPAND, HG_HEAD_DIM), _F32)
    blk_p = min(CHUNK, sp)
    y_p, g_p, h_p = trunk(x_prompt, zg, zh, FFN_ROWS, min(MIX_ROWS, sp), blk_p)
    y_s, g_s, h_s = trunk(x_sample, state_gla[0], state_hgrn[0], bs * ss, ss, ss)
    sd_g, sd_h = state_gla.dtype, state_hgrn.dtype
    return (y_p, y_s, g_p.astype(sd_g), h_p.astype(sd_h), g_s.astype(sd_g), h_s.astype(sd_h))
```

```python
import functools

import jax
import jax.numpy as jnp
from jax import lax
from jax.experimental import pallas as pl
from jax.experimental.pallas import tpu as pltpu

D_MODEL = 1024
CHUNK = 64
GLA_HEADS = 4
GLA_DK = 64
GLA_DV = 128
GLA_GATE_RANK = 16
GLA_GATE_NORM = 16.0
HG_HEADS = 4
HG_EXPAND = 128
HG_HEAD_DIM = 128
D_FF = 2816
EPS = 1e-6

LANES = 128
MXU_COLS = 256
N_HEADS = GLA_HEADS + HG_HEADS
FF_CHUNK = MXU_COLS
N_FF_CHUNKS = D_FF // FF_CHUNK
FFN_ROWS = 512
MIX_ROWS = 256
VMEM_LIMIT = 56 * 1024 * 1024
DECAY_GUARD = 60.0

_GD, _HF, _GQ, _GK, _HQ, _GV, _HI, _GR, _HGATE = 0, 256, 768, 1024, 1280, 1792, 2304, 2816, 3328
D_PROJ = _HGATE + HG_HEADS * HG_HEAD_DIM
PROJ_PIECE = 2 * MXU_COLS
N_PROJ_PIECES = -(-D_PROJ // PROJ_PIECE)
PROJ_SCHEDULE = (2, 2, 1, 1, 0, 1, 1)
assert sum(PROJ_SCHEDULE) == N_PROJ_PIECES

_F32 = jnp.float32
_BF16 = jnp.bfloat16


def _rmsnorm(x, g):
    return x * lax.rsqrt(jnp.mean(x * x, axis=-1, keepdims=True) + EPS) * g


def _sigmoid(x):
    return 1.0 / (1.0 + jnp.exp(-x))


def _dot(a, b):
    return jnp.dot(a, b, preferred_element_type=_F32)


_NT = (((1,), (1,)), ((), ()))
_TN = (((0,), (0,)), ((), ()))


def _dot_exact_rhs(sel, x):
    hi = x.astype(_BF16)
    r1 = x - hi.astype(_F32)
    mid = r1.astype(_BF16)
    lo = (r1 - mid.astype(_F32)).astype(_BF16)
    n = x.shape[1]
    parts = jnp.concatenate([hi, mid, lo], axis=1)
    r = jnp.dot(sel, parts, preferred_element_type=_F32)
    return r[:, :n] + r[:, n:2 * n] + r[:, 2 * n:]


def _ffn_kernel(x_ref, g_ref, wg_ref, wu_ref, wd_ref, gf_ref, o_ref, acc_ref, *, final_norm):
    x = x_ref[...]
    h = _rmsnorm(x, g_ref[...]).astype(_BF16)
    for c in range(N_FF_CHUNKS):
        cols = slice(c * FF_CHUNK, (c + 1) * FF_CHUNK)
        gate = _dot(h, wg_ref[:, cols])
        up = _dot(h, wu_ref[:, cols])
        a = (gate * _sigmoid(gate) * up).astype(_BF16)
        d = _dot(a, wd_ref[cols, :])
        if c == 0:
            acc_ref[...] = d
        else:
            acc_ref[...] += d
    y = x + 0.5 * acc_ref[...]
    if final_norm:
        y = _rmsnorm(y, gf_ref[...])
    o_ref[...] = y


def _const_spec(shape):
    zeros = (0,) * len(shape)
    return pl.BlockSpec(shape, lambda *_: zeros, pipeline_mode=pl.Buffered(1))


def _ffn(x2d, g, wg, wu, wd, gf, *, tm, final_norm):
    n = x2d.shape[0]
    assert n % tm == 0
    row_spec = pl.BlockSpec((tm, D_MODEL), lambda i: (i, 0))
    return pl.pallas_call(
        functools.partial(_ffn_kernel, final_norm=final_norm),
        grid=(n // tm,),
        in_specs=[row_spec, _const_spec(g.shape), _const_spec(wg.shape), _const_spec(wu.shape),
                  _const_spec(wd.shape), _const_spec(gf.shape)],
        out_specs=row_spec,
        out_shape=jax.ShapeDtypeStruct((n, D_MODEL), _F32),
        scratch_shapes=[pltpu.VMEM((tm, D_MODEL), _F32)],
        compiler_params=pltpu.CompilerParams(dimension_semantics=("arbitrary",),
                                             vmem_limit_bytes=VMEM_LIMIT),
        name="ffn_final" if final_norm else "ffn",
    )(x2d, g, wg, wu, wd, gf)


def _safe_scores(q, k, b, blk):
    gs = q.shape[0]
    ri = lax.broadcasted_iota(jnp.int32, (gs, gs), 0)
    ci = lax.broadcasted_iota(jnp.int32, (gs, gs), 1)
    qk = lax.dot_general(q.astype(_BF16), k.astype(_BF16), _NT, preferred_element_type=_F32)
    total = jnp.where(ri == ci, qk, 0.0)
    s = 1
    while s < blk:
        run = ri - ri % (2 * s)
        b_ref_rows = _dot_exact_rhs((ci == run + (s - 1)).astype(_BF16), b)
        qs = (q * jnp.exp(jnp.minimum(b - b_ref_rows, 0.0))).astype(_BF16)
        ks = (k * jnp.exp(jnp.minimum(b_ref_rows - b, 0.0))).astype(_BF16)
        level = (ri % (2 * s) >= s) & (ci % (2 * s) < s) & (run == ci - ci % (2 * s))
        total = total + jnp.where(level, lax.dot_general(qs, ks, _NT, preferred_element_type=_F32), 0.0)
        s *= 2
    return total


def _mixer_tile(pcur, st_from, st_to, x_ref, o_ref, mix_ref, gup_ref, gbias_ref, lb_ref, gon_ref, hon_ref, wout_ref,
                *, r0, tm, blk, fill, safe, st_keep=None):
    nb = tm // blk
    rows = slice(r0, r0 + tm)
    row = lax.broadcasted_iota(jnp.int32, (tm, tm), 0)
    col = lax.broadcasted_iota(jnp.int32, (tm, tm), 1)
    tri = ((col <= row) & (col >= row - row % blk)).astype(_BF16)
    lane = lax.broadcasted_iota(jnp.int32, (tm, LANES), 1)
    lb = lb_ref[...]

    def cat(x, y, axis):
        return jnp.concatenate([x, y], axis=axis)

    def cat_rows(parts):
        return parts[0] if len(parts) == 1 else jnp.concatenate(parts, axis=0)

    def bdiag(x, y):
        return cat(cat(x, jnp.zeros_like(y), 1), cat(jnp.zeros_like(x), y, 1), 0)

    gd = pcur(_GD, _GD + LANES).astype(_BF16)
    logits = jnp.dot(gd, gup_ref[...], preferred_element_type=_F32) + gbias_ref[...]
    g_cols = [(jnp.minimum(logits, 0.0) - jnp.log1p(jnp.exp(-jnp.abs(logits)))) / GLA_GATE_NORM]
    k_in = []
    for hd in range(HG_HEADS):
        z = pcur(_HF + hd * LANES, _HF + (hd + 1) * LANES)
        lb_h = lb[:, hd * LANES:(hd + 1) * LANES]
        e = jnp.exp(-jnp.abs(z))
        r = 1.0 / (1.0 + e)
        sig_pos = jnp.where(z >= 0, r, e * r)
        sig_neg = jnp.where(z >= 0, e * r, r)
        g_cols.append(jnp.log(lb_h + (1.0 - lb_h) * sig_pos))
        k_in.append((1.0 - lb_h) * sig_neg)
    fill(0)
    b_all = _dot_exact_rhs(tri, jnp.concatenate(g_cols, axis=1))
    b_min = b_all[blk - 1:blk, :]
    for c in range(1, nb):
        b_min = jnp.minimum(b_min, b_all[(c + 1) * blk - 1:(c + 1) * blk, :])
    b_min = jnp.min(jnp.broadcast_to(b_min, (8, b_min.shape[1])))

    def k_scale(b):
        if not safe:
            return jnp.exp(-b)
        b_last = [jnp.broadcast_to(b[(c + 1) * blk - 1:(c + 1) * blk, :], (blk, LANES)) for c in range(nb)]
        return jnp.exp(cat_rows(b_last) - b)

    qs, ks, bs, qts, kts, vs, ebs = [], [], [], [], [], [], []
    for p in range(GLA_HEADS // 2):
        b = b_all[:, p * LANES:(p + 1) * LANES]
        eb = jnp.exp(b)
        ek = k_scale(b)
        q = pcur(_GQ + p * LANES, _GQ + (p + 1) * LANES) * (GLA_DK ** -0.5)
        k = pcur(_GK + p * LANES, _GK + (p + 1) * LANES)
        for hh in range(2):
            idx = 2 * p + hh
            m = (lane >= hh * GLA_DK) & (lane < (hh + 1) * GLA_DK)
            qs.append(jnp.where(m, q, 0.0))
            ks.append(jnp.where(m, k, 0.0))
            bs.append(b)
            qts.append((qs[-1] * eb).astype(_BF16))
            kts.append((ks[-1] * ek).astype(_BF16))
            vs.append(pcur(_GV + idx * GLA_DV, _GV + (idx + 1) * GLA_DV).astype(_BF16))
            ebs.append(eb)
    for hd in range(HG_HEADS):
        b = b_all[:, (2 + hd) * LANES:(3 + hd) * LANES]
        eb = jnp.exp(b)
        hq = pcur(_HQ + hd * LANES, _HQ + (hd + 1) * LANES)
        qs.append(hq * _sigmoid(hq))
        ks.append(k_in[hd])
        bs.append(b)
        qts.append((qs[-1] * eb).astype(_BF16))
        kts.append((ks[-1] * k_scale(b)).astype(_BF16))
        vs.append(pcur(_HI + hd * LANES, _HI + (hd + 1) * LANES).astype(_BF16))
        ebs.append(eb)

    gs = min(tm, LANES)
    groups = [slice(u * gs, (u + 1) * gs) for u in range(tm // gs)]
    blocks = [slice(c * blk, (c + 1) * blk) for c in range(nb)]
    pairs = [(2 * p, 2 * p + 1) for p in range(N_HEADS // 2)]
    row2 = lax.broadcasted_iota(jnp.int32, (gs, 2 * gs), 0)
    col2 = lax.broadcasted_iota(jnp.int32, (gs, 2 * gs), 1)
    col2 = jnp.where(col2 >= gs, col2 - gs, col2)
    causal2 = (col2 <= row2) & (col2 >= row2 - row2 % blk)

    q2 = [cat(qts[a], qts[b], 1) for a, b in pairs]
    fill(1)
    kvs = [[lax.dot_general(cat(vs[a][rs], vs[b][rs], 0), bdiag(kts[a][rs], kts[b][rs]), _TN,
                            preferred_element_type=_F32)
            for rs in blocks] for a, b in pairs]

    if safe:
        scores = [[cat(_safe_scores(qs[a][g], ks[a][g], bs[a][g], blk),
                       _safe_scores(qs[b][g], ks[b][g], bs[b][g], blk), 1) for g in groups] for a, b in pairs]
    else:
        scores = [[lax.dot_general(q2[p][g], bdiag(kts[a][g], kts[b][g]), _NT, preferred_element_type=_F32)
                   for g in groups] for p, (a, b) in enumerate(pairs)]
    fill(2)
    o_intra = [[jnp.dot(jnp.where(causal2, scores[p][u], 0.0).astype(_BF16), bdiag(vs[a][g], vs[b][g]),
                        preferred_element_type=_F32)
                for u, g in enumerate(groups)] for p, (a, b) in enumerate(pairs)]
    fill(3)
    states = [cat(st_from[a], st_from[b], 1) for a, b in pairs]
    if st_keep is not None:
        for h in range(N_HEADS):
            st_keep[h] = states[h // 2][:, (h % 2) * LANES:(h % 2 + 1) * LANES]
    e2 = [cat(ebs[a], ebs[b], 1) for a, b in pairs]
    o_inter = [[] for _ in pairs]
    for c, rs in enumerate(blocks):
        for p in range(len(pairs)):
            s2 = states[p].astype(_BF16)
            o_inter[p].append(lax.dot_general(q2[p][rs], bdiag(s2[:, :LANES], s2[:, LANES:]), _NT,
                                              preferred_element_type=_F32))
        for p in range(len(pairs)):
            e_last = e2[p][(c + 1) * blk - 1:(c + 1) * blk, :]
            states[p] = states[p] * e_last + kvs[p][c] if safe else (states[p] + kvs[p][c]) * e_last
    fill(4)
    for p, (a, b) in enumerate(pairs):
        st_to[a] = states[p][:, :LANES]
        st_to[b] = states[p][:, LANES:]
        o2 = cat_rows(o_intra[p]) + cat_rows(o_inter[p])
        for hh, h in enumerate((a, b)):
            o = o2[:, hh * LANES:(hh + 1) * LANES]
            onorm = gon_ref[...] if h < GLA_HEADS else hon_ref[...]
            gate_col = _GR + h * LANES if h < GLA_HEADS else _HGATE + (h - GLA_HEADS) * LANES
            gate = pcur(gate_col, gate_col + LANES)
            o = o * lax.rsqrt(jnp.mean(o * o, axis=-1, keepdims=True) + EPS) * onorm
            o = o * (gate * _sigmoid(gate))
            mix_ref[rows, h * LANES:(h + 1) * LANES] = o.astype(_BF16)
    fill(5)
    out = jnp.dot(mix_ref[rows, :], wout_ref[...], preferred_element_type=_F32)
    fill(6)
    o_ref[rows, :] = x_ref[rows, :] + out
    return b_min


def _mixer_kernel(x_ref, xn_ref, st_in_ref, g_ref, win_ref, gup_ref, gbias_ref, lb_ref, gon_ref, hon_ref, wout_ref,
                  o_ref, st_ref, pa_ref, pb_ref, ha_ref, hb_ref, mix_ref, s0_ref, s1_ref, *, tm, blk, pipelined):
    u = pl.program_id(1)
    tile = functools.partial(_mixer_tile, x_ref=x_ref, o_ref=o_ref, mix_ref=mix_ref, gup_ref=gup_ref,
                             gbias_ref=gbias_ref, lb_ref=lb_ref, gon_ref=gon_ref, hon_ref=hon_ref,
                             wout_ref=wout_ref, tm=tm, blk=blk)
    no_fill = lambda stage: None

    def normed(src_ref, r0):
        return _rmsnorm(src_ref[r0:r0 + tm, :], g_ref[...]).astype(_BF16)

    valid = {id(pa_ref): set(range(N_PROJ_PIECES)), id(pb_ref): set()}

    def project(h, dst_ref, pieces, for_reader):
        for i in pieces:
            cols = slice(i * PROJ_PIECE, min((i + 1) * PROJ_PIECE, D_PROJ))
            dst_ref[:, cols] = jnp.dot(h, win_ref[:, cols], preferred_element_type=_F32)
            (valid[id(dst_ref)].add if for_reader else valid[id(dst_ref)].discard)(i)

    def reader(buf_ref):
        def read(c0, c1):
            assert all(i in valid[id(buf_ref)] for i in range(c0 // PROJ_PIECE, (c1 - 1) // PROJ_PIECE + 1))
            return buf_ref[:, c0:c1]
        return read

    def filler(h_ref, dst_ref, for_reader):
        pieces = iter(range(N_PROJ_PIECES))
        return lambda stage: project(h_ref[...], dst_ref, [next(pieces) for _ in range(PROJ_SCHEDULE[stage])],
                                     for_reader)

    @pl.when(u == 0)
    def _():
        for h in range(N_HEADS):
            st_ref[h] = st_in_ref[h].T

    if pipelined:
        @pl.when((pl.program_id(0) == 0) & (u == 0))
        def _():
            project(normed(x_ref, 0), pa_ref, range(N_PROJ_PIECES), True)
            hb_ref[...] = normed(x_ref, tm)

        ha_ref[...] = normed(xn_ref, 0)
        low_a = tile(reader(pa_ref), st_ref, s1_ref, r0=0, fill=filler(hb_ref, pb_ref, True), safe=False,
                     st_keep=s0_ref)
        low_b = tile(reader(pb_ref), s1_ref, st_ref, r0=tm, fill=filler(ha_ref, pa_ref, False), safe=False)
        hb_ref[...] = normed(xn_ref, tm)

        @pl.when(low_a < -DECAY_GUARD)
        def _():
            project(normed(x_ref, 0), pb_ref, range(N_PROJ_PIECES), True)
            tile(reader(pb_ref), s0_ref, s1_ref, r0=0, fill=no_fill, safe=True)
            project(normed(x_ref, tm), pb_ref, range(N_PROJ_PIECES), True)

        @pl.when(jnp.minimum(low_a, low_b) < -DECAY_GUARD)
        def _():
            tile(reader(pb_ref), s1_ref, st_ref, r0=tm, fill=no_fill, safe=True)
    else:
        project(normed(x_ref, 0), pa_ref, range(N_PROJ_PIECES), True)
        low_a = tile(reader(pa_ref), st_ref, st_ref, r0=0, fill=no_fill, safe=False, st_keep=s0_ref)

        @pl.when(low_a < -DECAY_GUARD)
        def _():
            tile(reader(pa_ref), s0_ref, st_ref, r0=0, fill=no_fill, safe=True)


    @pl.when(u == pl.num_programs(1) - 1)
    def _():
        for h in range(N_HEADS):
            st_ref[h] = st_ref[h].T


def _mixer(x, st_in, g, win, gup, gbias, lb, gon, hon, wout, *, tm, blk):
    bsz, seq, _ = x.shape
    pipelined = seq % (2 * tm) == 0
    rows = 2 * tm if pipelined else tm
    assert seq % rows == 0 and tm % blk == 0 and blk & (blk - 1) == 0
    nu = seq // rows
    last = bsz * nu - 1

    def next_block(b, u):
        nxt = jnp.minimum(b * nu + u + 1, last)
        return (nxt // nu, nxt % nu, 0)

    row_spec = pl.BlockSpec((None, rows, D_MODEL), lambda b, u: (b, u, 0))
    st_spec = pl.BlockSpec((None, N_HEADS, LANES, LANES), lambda b, u: (b, 0, 0, 0))
    return pl.pallas_call(
        functools.partial(_mixer_kernel, tm=tm, blk=blk, pipelined=pipelined),
        grid=(bsz, nu),
        in_specs=[row_spec, pl.BlockSpec((None, rows, D_MODEL), next_block), st_spec]
        + [_const_spec(a.shape) for a in (g, win, gup, gbias, lb, gon, hon, wout)],
        out_specs=[row_spec, st_spec],
        out_shape=[jax.ShapeDtypeStruct(x.shape, _F32),
                   jax.ShapeDtypeStruct((bsz, N_HEADS, LANES, LANES), _F32)],
        scratch_shapes=[pltpu.VMEM((tm, D_PROJ), _F32), pltpu.VMEM((tm, D_PROJ), _F32),
                        pltpu.VMEM((tm, D_MODEL), _BF16), pltpu.VMEM((tm, D_MODEL), _BF16),
                        pltpu.VMEM((rows, D_MODEL), _BF16),
                        pltpu.VMEM((N_HEADS, LANES, LANES), _F32), pltpu.VMEM((N_HEADS, LANES, LANES), _F32)],
        compiler_params=pltpu.CompilerParams(dimension_semantics=("arbitrary", "arbitrary"),
                                             vmem_limit_bytes=VMEM_LIMIT),
        name="mixer",
    )(x, x, st_in, g, win, gup, gbias, lb, gon, hon, wout)


def _pack_state(s_gla, s_hg):
    zeros = jnp.zeros_like(s_gla)
    even = jnp.concatenate([s_gla, zeros], axis=2)
    odd = jnp.concatenate([zeros, s_gla], axis=2)
    is_odd = (jnp.arange(GLA_HEADS) % 2 == 1)[None, :, None, None]
    gla = jnp.where(is_odd, odd, even)
    return jnp.concatenate([gla, s_hg], axis=1).astype(_F32)


def _unpack_state(st):
    gla = st[:, :GLA_HEADS]
    is_odd = (jnp.arange(GLA_HEADS) % 2 == 1)[None, :, None, None]
    gla = jnp.where(is_odd, gla[:, :, GLA_DK:], gla[:, :, :GLA_DK])
    return gla, st[:, GLA_HEADS:]


def kernel(x_prompt, x_sample, state_gla, state_hgrn, norm_ffn1, ffn1_w_gate, ffn1_w_up, ffn1_w_down, norm_mix, w_in, gla_gate_up, gla_gate_bias, gla_onorm, hg_lower_bound_logits, hg_onorm, w_out, norm_ffn2, ffn2_w_gate, ffn2_w_up, ffn2_w_down, norm_final):
    assert norm_ffn1.shape[0] == 1, "single layer"
    row = lambda v: v.reshape(1, -1).astype(_F32)
    bf = lambda w: w.astype(_BF16)
    f1 = (row(norm_ffn1[0]), bf(ffn1_w_gate[0]), bf(ffn1_w_up[0]), bf(ffn1_w_down[0]), row(norm_final))
    f2 = (row(norm_ffn2[0]), bf(ffn2_w_gate[0]), bf(ffn2_w_up[0]), bf(ffn2_w_down[0]), row(norm_final))
    sizes = (GLA_HEADS * GLA_DK, GLA_HEADS * GLA_DK, GLA_HEADS * GLA_DV, GLA_HEADS * GLA_DV, GLA_GATE_RANK,
             HG_HEADS * HG_EXPAND, HG_HEADS * HG_EXPAND, HG_HEADS * HG_HEAD_DIM, HG_HEADS * HG_HEAD_DIM)
    splits = [sum(sizes[:i + 1]) for i in range(len(sizes) - 1)]
    gq, gk, gv, gr, gd, hq, hf, hi, hgate = jnp.split(w_in[0], splits, axis=1)
    gd = jnp.concatenate([gd, jnp.zeros((D_MODEL, _HF - GLA_GATE_RANK), gd.dtype)], axis=1)
    win = jnp.concatenate([gd, hf, gq, gk, hq, gv, hi, gr, hgate], axis=1).astype(_BF16)
    gup = jnp.concatenate([gla_gate_up[0], jnp.zeros((LANES - GLA_GATE_RANK, GLA_HEADS * GLA_DK), _F32)],
                          axis=0).astype(_BF16)
    lb = jnp.cumsum(jax.nn.softmax(hg_lower_bound_logits.astype(_F32), axis=0), axis=0)[0]
    mix_w = (row(norm_mix[0]), win, gup, row(gla_gate_bias[0]), row(lb), row(gla_onorm[0]), row(hg_onorm[0]),
             bf(w_out[0]))

    def trunk(x, s_gla, s_hg, tm_ffn, tm_mix, blk):
        bsz, seq, _ = x.shape
        x1 = _ffn(x.reshape(bsz * seq, D_MODEL), *f1, tm=tm_ffn, final_norm=False)
        x2, st = _mixer(x1.reshape(bsz, seq, D_MODEL), _pack_state(s_gla, s_hg), *mix_w, tm=tm_mix, blk=blk)
        y = _ffn(x2.reshape(bsz * seq, D_MODEL), *f2, tm=tm_ffn, final_norm=True)
        g, h = _unpack_state(st)
        return y.reshape(bsz, seq, D_MODEL), g[None], h[None]

    bp, sp, _ = x_prompt.shape
    bs, ss, _ = x_sample.shape
    zg = jnp.zeros((bp, GLA_HEADS, GLA_DK, GLA_DV), _F32)
    zh = jnp.zeros((bp, HG_HEADS, HG_EXPAND, HG_HEAD_DIM), _F32)
    blk_p = min(CHUNK, sp)
    y_p, g_p, h_p = trunk(x_prompt, zg, zh, FFN_ROWS, min(MIX_ROWS, sp), blk_p)
    y_s, g_s, h_s = trunk(x_sample, state_gla[0], state_hgrn[0], bs * ss, ss, ss)
    sd_g, sd_h = state_gla.dtype, state_hgrn.dtype
    return (y_p, y_s, g_p.astype(sd_g), h_p.astype(sd_h), g_s.astype(sd_g), h_s.astype(sd_h))
```

```python
import functools

import jax
import jax.numpy as jnp
from jax import lax
from jax.experimental import pallas as pl
from jax.experimental.pallas import tpu as pltpu

D_MODEL = 1024
CHUNK = 64
GLA_HEADS = 4
GLA_DK = 64
GLA_DV = 128
GLA_GATE_RANK = 16
GLA_GATE_NORM = 16.0
HG_HEADS = 4
HG_EXPAND = 128
HG_HEAD_DIM = 128
D_FF = 2816
EPS = 1e-6

LANES = 128
MXU_COLS = 256
N_HEADS = GLA_HEADS + HG_HEADS
FF_CHUNK = MXU_COLS
N_FF_CHUNKS = D_FF // FF_CHUNK
FFN_ROWS = 512
MIX_ROWS = 256
VMEM_LIMIT = 56 * 1024 * 1024
DECAY_GUARD = 60.0

_GD, _HF, _GQ, _GK, _HQ, _GV, _HI, _GR, _HGATE = 0, 256, 768, 1024, 1280, 1792, 2304, 2816, 3328
D_PROJ = _HGATE + HG_HEADS * HG_HEAD_DIM
PROJ_PIECE = 2 * MXU_COLS
N_PROJ_PIECES = -(-D_PROJ // PROJ_PIECE)
PROJ_SCHEDULE = (2, 2, 1, 1, 0, 1, 1)
assert sum(PROJ_SCHEDULE) == N_PROJ_PIECES

_F32 = jnp.float32
_BF16 = jnp.bfloat16


def _rmsnorm(x, g):
    return x * lax.rsqrt(jnp.mean(x * x, axis=-1, keepdims=True) + EPS) * g


def _sigmoid(x):
    return 1.0 / (1.0 + jnp.exp(-x))


def _dot(a, b):
    return jnp.dot(a, b, preferred_element_type=_F32)


_NT = (((1,), (1,)), ((), ()))
_TN = (((0,), (0,)), ((), ()))


def _dot_exact_rhs(sel, x):
    hi = x.astype(_BF16)
    r1 = x - hi.astype(_F32)
    mid = r1.astype(_BF16)
    lo = (r1 - mid.astype(_F32)).astype(_BF16)
    n = x.shape[1]
    parts = jnp.concatenate([hi, mid, lo], axis=1)
    r = jnp.dot(sel, parts, preferred_element_type=_F32)
    return r[:, :n] + r[:, n:2 * n] + r[:, 2 * n:]


def _ffn_kernel(x_ref, g_ref, wg_ref, wu_ref, wd_ref, gf_ref, o_ref, acc_ref, *, final_norm):
    x = x_ref[...]
    h = _rmsnorm(x, g_ref[...]).astype(_BF16)
    for c in range(N_FF_CHUNKS):
        cols = slice(c * FF_CHUNK, (c + 1) * FF_CHUNK)
        gate = _dot(h, wg_ref[:, cols])
        up = _dot(h, wu_ref[:, cols])
        a = (gate * _sigmoid(gate) * up).astype(_BF16)
        d = _dot(a, wd_ref[cols, :])
        if c == 0:
            acc_ref[...] = d
        else:
            acc_ref[...] += d
    y = x + 0.5 * acc_ref[...]
    if final_norm:
        y = _rmsnorm(y, gf_ref[...])
    o_ref[...] = y


def _const_spec(shape):
    zeros = (0,) * len(shape)
    return pl.BlockSpec(shape, lambda *_: zeros, pipeline_mode=pl.Buffered(1))


def _ffn(x2d, g, wg, wu, wd, gf, *, tm, final_norm):
    n = x2d.shape[0]
    assert n % tm == 0
    row_spec = pl.BlockSpec((tm, D_MODEL), lambda i: (i, 0))
    return pl.pallas_call(
        functools.partial(_ffn_kernel, final_norm=final_norm),
        grid=(n // tm,),
        in_specs=[row_spec, _const_spec(g.shape), _const_spec(wg.shape), _const_spec(wu.shape),
                  _const_spec(wd.shape), _const_spec(gf.shape)],
        out_specs=row_spec,
        out_shape=jax.ShapeDtypeStruct((n, D_MODEL), _F32),
        scratch_shapes=[pltpu.VMEM((tm, D_MODEL), _F32)],
        compiler_params=pltpu.CompilerParams(dimension_semantics=("arbitrary",),
                                             vmem_limit_bytes=VMEM_LIMIT),
        name="ffn_final" if final_norm else "ffn",
    )(x2d, g, wg, wu, wd, gf)


def _safe_scores(q, k, b, blk):
    gs = q.shape[0]
    ri = lax.broadcasted_iota(jnp.int32, (gs, gs), 0)
    ci = lax.broadcasted_iota(jnp.int32, (gs, gs), 1)
    qk = lax.dot_general(q.astype(_BF16), k.astype(_BF16), _NT, preferred_element_type=_F32)
    total = jnp.where(ri == ci, qk, 0.0)
    s = 1
    while s < blk:
        run = ri - ri % (2 * s)
        b_ref_rows = _dot_exact_rhs((ci == run + (s - 1)).astype(_BF16), b)
        qs = (q * jnp.exp(jnp.minimum(b - b_ref_rows, 0.0))).astype(_BF16)
        ks = (k * jnp.exp(jnp.minimum(b_ref_rows - b, 0.0))).astype(_BF16)
        level = (ri % (2 * s) >= s) & (ci % (2 * s) < s) & (run == ci - ci % (2 * s))
        total = total + jnp.where(level, lax.dot_general(qs, ks, _NT, preferred_element_type=_F32), 0.0)
        s *= 2
    return total


def _mixer_tile(pcur, st_from, st_to, x_ref, o_ref, mix_ref, gup_ref, gbias_ref, lb_ref, gon_ref, hon_ref, wout_ref,
                *, r0, tm, blk, fill, safe, st_keep=None):
    nb = tm // blk
    rows = pl.ds(r0, tm)
    row = lax.broadcasted_iota(jnp.int32, (tm, tm), 0)
    col = lax.broadcasted_iota(jnp.int32, (tm, tm), 1)
    tri = ((col <= row) & (col >= row - row % blk)).astype(_BF16)
    lane = lax.broadcasted_iota(jnp.int32, (tm, LANES), 1)
    lb = lb_ref[...]

    def cat(x, y, axis):
        return jnp.concatenate([x, y], axis=axis)

    def cat_rows(parts):
        return parts[0] if len(parts) == 1 else jnp.concatenate(parts, axis=0)

    def bdiag(x, y):
        return cat(cat(x, jnp.zeros_like(y), 1), cat(jnp.zeros_like(x), y, 1), 0)

    gd = pcur(_GD, _GD + LANES).astype(_BF16)
    logits = jnp.dot(gd, gup_ref[...], preferred_element_type=_F32) + gbias_ref[...]
    g_cols = [(jnp.minimum(logits, 0.0) - jnp.log1p(jnp.exp(-jnp.abs(logits)))) / GLA_GATE_NORM]
    k_in = []
    for hd in range(HG_HEADS):
        z = pcur(_HF + hd * LANES, _HF + (hd + 1) * LANES)
        lb_h = lb[:, hd * LANES:(hd + 1) * LANES]
        e = jnp.exp(-jnp.abs(z))
        r = 1.0 / (1.0 + e)
        sig_pos = jnp.where(z >= 0, r, e * r)
        sig_neg = jnp.where(z >= 0, e * r, r)
        g_cols.append(jnp.log(lb_h + (1.0 - lb_h) * sig_pos))
        k_in.append((1.0 - lb_h) * sig_neg)
    fill(0)
    b_all = _dot_exact_rhs(tri, jnp.concatenate(g_cols, axis=1))
    b_min = b_all[blk - 1:blk, :]
    for c in range(1, nb):
        b_min = jnp.minimum(b_min, b_all[(c + 1) * blk - 1:(c + 1) * blk, :])
    b_min = jnp.min(jnp.broadcast_to(b_min, (8, b_min.shape[1])))

    def k_scale(b):
        if not safe:
            return jnp.exp(-b)
        b_last = [jnp.broadcast_to(b[(c + 1) * blk - 1:(c + 1) * blk, :], (blk, LANES)) for c in range(nb)]
        return jnp.exp(cat_rows(b_last) - b)

    qs, ks, bs, qts, kts, vs, ebs = [], [], [], [], [], [], []
    for p in range(GLA_HEADS // 2):
        b = b_all[:, p * LANES:(p + 1) * LANES]
        eb = jnp.exp(b)
        ek = k_scale(b)
        q = pcur(_GQ + p * LANES, _GQ + (p + 1) * LANES) * (GLA_DK ** -0.5)
        k = pcur(_GK + p * LANES, _GK + (p + 1) * LANES)
        for hh in range(2):
            idx = 2 * p + hh
            m = (lane >= hh * GLA_DK) & (lane < (hh + 1) * GLA_DK)
            qs.append(jnp.where(m, q, 0.0))
            ks.append(jnp.where(m, k, 0.0))
            bs.append(b)
            qts.append((qs[-1] * eb).astype(_BF16))
            kts.append((ks[-1] * ek).astype(_BF16))
            vs.append(pcur(_GV + idx * GLA_DV, _GV + (idx + 1) * GLA_DV).astype(_BF16))
            ebs.append(eb)
    for hd in range(HG_HEADS):
        b = b_all[:, (2 + hd) * LANES:(3 + hd) * LANES]
        eb = jnp.exp(b)
        hq = pcur(_HQ + hd * LANES, _HQ + (hd + 1) * LANES)
        qs.append(hq * _sigmoid(hq))
        ks.append(k_in[hd])
        bs.append(b)
        qts.append((qs[-1] * eb).astype(_BF16))
        kts.append((ks[-1] * k_scale(b)).astype(_BF16))
        vs.append(pcur(_HI + hd * LANES, _HI + (hd + 1) * LANES).astype(_BF16))
        ebs.append(eb)

    gs = min(tm, LANES)
    groups = [slice(u * gs, (u + 1) * gs) for u in range(tm // gs)]
    blocks = [slice(c * blk, (c + 1) * blk) for c in range(nb)]
    pairs = [(2 * p, 2 * p + 1) for p in range(N_HEADS // 2)]
    row2 = lax.broadcasted_iota(jnp.int32, (gs, 2 * gs), 0)
    col2 = lax.broadcasted_iota(jnp.int32, (gs, 2 * gs), 1)
    col2 = jnp.where(col2 >= gs, col2 - gs, col2)
    causal2 = (col2 <= row2) & (col2 >= row2 - row2 % blk)

    q2 = [cat(qts[a], qts[b], 1) for a, b in pairs]
    fill(1)
    kvs = [[lax.dot_general(cat(vs[a][rs], vs[b][rs], 0), bdiag(kts[a][rs], kts[b][rs]), _TN,
                            preferred_element_type=_F32)
            for rs in blocks] for a, b in pairs]

    if safe:
        scores = [[cat(_safe_scores(qs[a][g], ks[a][g], bs[a][g], blk),
                       _safe_scores(qs[b][g], ks[b][g], bs[b][g], blk), 1) for g in groups] for a, b in pairs]
    else:
        scores = [[lax.dot_general(q2[p][g], bdiag(kts[a][g], kts[b][g]), _NT, preferred_element_type=_F32)
                   for g in groups] for p, (a, b) in enumerate(pairs)]
    fill(2)
    o_intra = [[jnp.dot(jnp.where(causal2, scores[p][u], 0.0).astype(_BF16), bdiag(vs[a][g], vs[b][g]),
                        preferred_element_type=_F32)
                for u, g in enumerate(groups)] for p, (a, b) in enumerate(pairs)]
    fill(3)
    states = [cat(st_from[a], st_from[b], 1) for a, b in pairs]
    if st_keep is not None:
        for h in range(N_HEADS):
            st_keep[h] = states[h // 2][:, (h % 2) * LANES:(h % 2 + 1) * LANES]
    e2 = [cat(ebs[a], ebs[b], 1) for a, b in pairs]
    o_inter = [[] for _ in pairs]
    for c, rs in enumerate(blocks):
        for p in range(len(pairs)):
            s2 = states[p].astype(_BF16)
            o_inter[p].append(lax.dot_general(q2[p][rs], bdiag(s2[:, :LANES], s2[:, LANES:]), _NT,
                                              preferred_element_type=_F32))
        for p in range(len(pairs)):
            e_last = e2[p][(c + 1) * blk - 1:(c + 1) * blk, :]
            states[p] = states[p] * e_last + kvs[p][c] if safe else (states[p] + kvs[p][c]) * e_last
    fill(4)
    for p, (a, b) in enumerate(pairs):
        st_to[a] = states[p][:, :LANES]
        st_to[b] = states[p][:, LANES:]
        o2 = cat_rows(o_intra[p]) + cat_rows(o_inter[p])
        for hh, h in enumerate((a, b)):
            o = o2[:, hh * LANES:(hh + 1) * LANES]
            onorm = gon_ref[...] if h < GLA_HEADS else hon_ref[...]
            gate_col = _GR + h * LANES if h < GLA_HEADS else _HGATE + (h - GLA_HEADS) * LANES
            gate = pcur(gate_col, gate_col + LANES)
            o = o * lax.rsqrt(jnp.mean(o * o, axis=-1, keepdims=True) + EPS) * onorm
            o = o * (gate * _sigmoid(gate))
            mix_ref[rows, h * LANES:(h + 1) * LANES] = o.astype(_BF16)
    fill(5)
    out = jnp.dot(mix_ref[rows, :], wout_ref[...], preferred_element_type=_F32)
    fill(6)
    o_ref[rows, :] = x_ref[rows, :] + out
    return b_min


def _mixer_kernel(x_ref, xn_ref, st_in_ref, g_ref, win_ref, gup_ref, gbias_ref, lb_ref, gon_ref, hon_ref, wout_ref,
                  o_ref, st_ref, pa_ref, pb_ref, ha_ref, hb_ref, mix_ref, sk_ref, *, tm, blk, pipelined):
    u = pl.program_id(1)
    tile = functools.partial(_mixer_tile, x_ref=x_ref, o_ref=o_ref, mix_ref=mix_ref, gup_ref=gup_ref,
                             gbias_ref=gbias_ref, lb_ref=lb_ref, gon_ref=gon_ref, hon_ref=hon_ref,
                             wout_ref=wout_ref, tm=tm, blk=blk)
    no_fill = lambda stage: None

    def normed(src_ref, r0):
        return _rmsnorm(src_ref[pl.ds(r0, tm), :], g_ref[...]).astype(_BF16)

    valid = {id(pa_ref): set(range(N_PROJ_PIECES)), id(pb_ref): set()}

    def project(h, dst_ref, pieces, for_reader):
        for i in pieces:
            cols = slice(i * PROJ_PIECE, min((i + 1) * PROJ_PIECE, D_PROJ))
            dst_ref[:, cols] = jnp.dot(h, win_ref[:, cols], preferred_element_type=_F32)
            (valid[id(dst_ref)].add if for_reader else valid[id(dst_ref)].discard)(i)

    def reader(buf_ref):
        def read(c0, c1):
            assert all(i in valid[id(buf_ref)] for i in range(c0 // PROJ_PIECE, (c1 - 1) // PROJ_PIECE + 1))
            return buf_ref[:, c0:c1]
        return read

    def filler(h_ref, dst_ref, for_reader):
        pieces = iter(range(N_PROJ_PIECES))
        return lambda stage: project(h_ref[...], dst_ref, [next(pieces) for _ in range(PROJ_SCHEDULE[stage])],
                                     for_reader)

    @pl.when(u == 0)
    def _():
        for h in range(N_HEADS):
            st_ref[h] = st_in_ref[h].T

    if pipelined:
        @pl.when((pl.program_id(0) == 0) & (u == 0))
        def _():
            project(normed(x_ref, 0), pa_ref, range(N_PROJ_PIECES), True)
            hb_ref[...] = normed(x_ref, tm)

        ha_ref[...] = normed(xn_ref, 0)
        low_a = tile(reader(pa_ref), st_ref, sk_ref.at[1], r0=0, fill=filler(hb_ref, pb_ref, True), safe=False,
                     st_keep=sk_ref.at[0])
        low_b = tile(reader(pb_ref), sk_ref.at[1], st_ref, r0=tm, fill=filler(ha_ref, pa_ref, False), safe=False)
        hb_ref[...] = normed(xn_ref, tm)
        low_ab = jnp.minimum(low_a, low_b)

        def redo(i, carry):
            @pl.when(jnp.where(i == 0, low_a, low_ab) < -DECAY_GUARD)
            def _():
                r0 = pl.multiple_of(i * tm, tm)
                project(normed(x_ref, r0), pb_ref, range(N_PROJ_PIECES), True)
                tile(reader(pb_ref), sk_ref.at[i], sk_ref.at[i + 1], r0=r0, fill=no_fill, safe=True)
            return carry

        lax.fori_loop(0, 2, redo, 0)

        @pl.when(low_ab < -DECAY_GUARD)
        def _():
            for h in range(N_HEADS):
                st_ref[h] = sk_ref[2, h]
    else:
        project(normed(x_ref, 0), pa_ref, range(N_PROJ_PIECES), True)
        low_a = tile(reader(pa_ref), st_ref, st_ref, r0=0, fill=no_fill, safe=False, st_keep=sk_ref.at[0])

        @pl.when(low_a < -DECAY_GUARD)
        def _():
            tile(reader(pa_ref), sk_ref.at[0], st_ref, r0=0, fill=no_fill, safe=True)


    @pl.when(u == pl.num_programs(1) - 1)
    def _():
        for h in range(N_HEADS):
            st_ref[h] = st_ref[h].T


def _mixer(x, st_in, g, win, gup, gbias, lb, gon, hon, wout, *, tm, blk):
    bsz, seq, _ = x.shape
    pipelined = seq % (2 * tm) == 0
    rows = 2 * tm if pipelined else tm
    assert seq % rows == 0 and tm % blk == 0 and blk & (blk - 1) == 0
    nu = seq // rows
    last = bsz * nu - 1

    def next_block(b, u):
        nxt = jnp.minimum(b * nu + u + 1, last)
        return (nxt // nu, nxt % nu, 0)

    row_spec = pl.BlockSpec((None, rows, D_MODEL), lambda b, u: (b, u, 0))
    st_spec = pl.BlockSpec((None, N_HEADS, LANES, LANES), lambda b, u: (b, 0, 0, 0))
    return pl.pallas_call(
        functools.partial(_mixer_kernel, tm=tm, blk=blk, pipelined=pipelined),
        grid=(bsz, nu),
        in_specs=[row_spec, pl.BlockSpec((None, rows, D_MODEL), next_block), st_spec]
        + [_const_spec(a.shape) for a in (g, win, gup, gbias, lb, gon, hon, wout)],
        out_specs=[row_spec, st_spec],
        out_shape=[jax.ShapeDtypeStruct(x.shape, _F32),
                   jax.ShapeDtypeStruct((bsz, N_HEADS, LANES, LANES), _F32)],
        scratch_shapes=[pltpu.VMEM((tm, D_PROJ), _F32), pltpu.VMEM((tm, D_PROJ), _F32),
                        pltpu.VMEM((tm, D_MODEL), _BF16), pltpu.VMEM((tm, D_MODEL), _BF16),
                        pltpu.VMEM((rows, D_MODEL), _BF16),
                        pltpu.VMEM((3, N_HEADS, LANES, LANES), _F32)],
        compiler_params=pltpu.CompilerParams(dimension_semantics=("arbitrary", "arbitrary"),
                                             vmem_limit_bytes=VMEM_LIMIT),
        name="mixer",
    )(x, x, st_in, g, win, gup, gbias, lb, gon, hon, wout)


def _pack_state(s_gla, s_hg):
    zeros = jnp.zeros_like(s_gla)
    even = jnp.concatenate([s_gla, zeros], axis=2)
    odd = jnp.concatenate([zeros, s_gla], axis=2)
    is_odd = (jnp.arange(GLA_HEADS) % 2 == 1)[None, :, None, None]
    gla = jnp.where(is_odd, odd, even)
    return jnp.concatenate([gla, s_hg], axis=1).astype(_F32)


def _unpack_state(st):
    gla = st[:, :GLA_HEADS]
    is_odd = (jnp.arange(GLA_HEADS) % 2 == 1)[None, :, None, None]
    gla = jnp.where(is_odd, gla[:, :, GLA_DK:], gla[:, :, :GLA_DK])
    return gla, st[:, GLA_HEADS:]


def kernel(x_prompt, x_sample, state_gla, state_hgrn, norm_ffn1, ffn1_w_gate, ffn1_w_up, ffn1_w_down, norm_mix, w_in, gla_gate_up, gla_gate_bias, gla_onorm, hg_lower_bound_logits, hg_onorm, w_out, norm_ffn2, ffn2_w_gate, ffn2_w_up, ffn2_w_down, norm_final):
    assert norm_ffn1.shape[0] == 1, "single layer"
    row = lambda v: v.reshape(1, -1).astype(_F32)
    bf = lambda w: w.astype(_BF16)
    f1 = (row(norm_ffn1[0]), bf(ffn1_w_gate[0]), bf(ffn1_w_up[0]), bf(ffn1_w_down[0]), row(norm_final))
    f2 = (row(norm_ffn2[0]), bf(ffn2_w_gate[0]), bf(ffn2_w_up[0]), bf(ffn2_w_down[0]), row(norm_final))
    sizes = (GLA_HEADS * GLA_DK, GLA_HEADS * GLA_DK, GLA_HEADS * GLA_DV, GLA_HEADS * GLA_DV, GLA_GATE_RANK,
             HG_HEADS * HG_EXPAND, HG_HEADS * HG_EXPAND, HG_HEADS * HG_HEAD_DIM, HG_HEADS * HG_HEAD_DIM)
    splits = [sum(sizes[:i + 1]) for i in range(len(sizes) - 1)]
    gq, gk, gv, gr, gd, hq, hf, hi, hgate = jnp.split(w_in[0], splits, axis=1)
    gd = jnp.concatenate([gd, jnp.zeros((D_MODEL, _HF - GLA_GATE_RANK), gd.dtype)], axis=1)
    win = jnp.concatenate([gd, hf, gq, gk, hq, gv, hi, gr, hgate], axis=1).astype(_BF16)
    gup = jnp.concatenate([gla_gate_up[0], jnp.zeros((LANES - GLA_GATE_RANK, GLA_HEADS * GLA_DK), _F32)],
                          axis=0).astype(_BF16)
    lb = jnp.cumsum(jax.nn.softmax(hg_lower_bound_logits.astype(_F32), axis=0), axis=0)[0]
    mix_w = (row(norm_mix[0]), win, gup, row(gla_gate_bias[0]), row(lb), row(gla_onorm[0]), row(hg_onorm[0]),
             bf(w_out[0]))

    def trunk(x, s_gla, s_hg, tm_ffn, tm_mix, blk):
        bsz, seq, _ = x.shape
        x1 = _ffn(x.reshape(bsz * seq, D_MODEL), *f1, tm=tm_ffn, final_norm=False)
        x2, st = _mixer(x1.reshape(bsz, seq, D_MODEL), _pack_state(s_gla, s_hg), *mix_w, tm=tm_mix, blk=blk)
        y = _ffn(x2.reshape(bsz * seq, D_MODEL), *f2, tm=tm_ffn, final_norm=True)
        g, h = _unpack_state(st)
        return y.reshape(bsz, seq, D_MODEL), g[None], h[None]

    bp, sp, _ = x_prompt.shape
    bs, ss, _ = x_sample.shape
    zg = jnp.zeros((bp, GLA_HEADS, GLA_DK, GLA_DV), _F32)
    zh = jnp.zeros((bp, HG_HEADS, HG_EXPAND, HG_HEAD_DIM), _F32)
    blk_p = min(CHUNK, sp)
    y_p, g_p, h_p = trunk(x_prompt, zg, zh, FFN_ROWS, min(MIX_ROWS, sp), blk_p)
    y_s, g_s, h_s = trunk(x_sample, state_gla[0], state_hgrn[0], bs * ss, ss, ss)
    sd_g, sd_h = state_gla.dtype, state_hgrn.dtype
    return (y_p, y_s, g_p.astype(sd_g), h_p.astype(sd_h), g_s.astype(sd_g), h_s.astype(sd_h))
```

```python
import functools

import jax
import jax.numpy as jnp
from jax import lax
from jax.experimental import pallas as pl
from jax.experimental.pallas import tpu as pltpu

D_MODEL = 1024
CHUNK = 64
GLA_HEADS = 4
GLA_DK = 64
GLA_DV = 128
GLA_GATE_RANK = 16
GLA_GATE_NORM = 16.0
HG_HEADS = 4
HG_EXPAND = 128
HG_HEAD_DIM = 128
D_FF = 2816
EPS = 1e-6

LANES = 128
MXU_COLS = 256
N_HEADS = GLA_HEADS + HG_HEADS
FF_CHUNK = MXU_COLS
N_FF_CHUNKS = D_FF // FF_CHUNK
FFN_ROWS = 512
MIX_ROWS = 256
VMEM_LIMIT = 56 * 1024 * 1024
DECAY_GUARD = 60.0

_GD, _HF, _GQ, _GK, _HQ, _GV, _HI, _GR, _HGATE = 0, 256, 768, 1024, 1280, 1792, 2304, 2816, 3328
D_PROJ = _HGATE + HG_HEADS * HG_HEAD_DIM
PROJ_PIECE = 2 * MXU_COLS
N_PROJ_PIECES = -(-D_PROJ // PROJ_PIECE)
PROJ_SCHEDULE = (2, 2, 1, 1, 1, 1, 0)
assert sum(PROJ_SCHEDULE) == N_PROJ_PIECES

_F32 = jnp.float32
_BF16 = jnp.bfloat16


def _rmsnorm(x, g):
    return x * lax.rsqrt(jnp.mean(x * x, axis=-1, keepdims=True) + EPS) * g


def _sigmoid(x):
    return 1.0 / (1.0 + jnp.exp(-x))


def _dot(a, b):
    return jnp.dot(a, b, preferred_element_type=_F32)


_NT = (((1,), (1,)), ((), ()))
_TN = (((0,), (0,)), ((), ()))


def _dot_split_rhs(sel, x, n_parts):
    parts, rest = [], x
    for _ in range(n_parts):
        parts.append(rest.astype(_BF16))
        rest = rest - parts[-1].astype(_F32)
    n = x.shape[1]
    r = jnp.dot(sel, jnp.concatenate(parts, axis=1), preferred_element_type=_F32)
    return sum(r[:, i * n:(i + 1) * n] for i in range(1, n_parts)) + r[:, :n]


def _ffn_kernel(x_ref, g_ref, wg_ref, wu_ref, wd_ref, gf_ref, o_ref, acc_ref, *, final_norm):
    x = x_ref[...]
    h = _rmsnorm(x, g_ref[...]).astype(_BF16)
    for c in range(N_FF_CHUNKS):
        cols = slice(c * FF_CHUNK, (c + 1) * FF_CHUNK)
        gate = _dot(h, wg_ref[:, cols])
        up = _dot(h, wu_ref[:, cols])
        a = (gate * _sigmoid(gate) * up).astype(_BF16)
        d = _dot(a, wd_ref[cols, :])
        if c == 0:
            acc_ref[...] = d
        else:
            acc_ref[...] += d
    y = x + 0.5 * acc_ref[...]
    if final_norm:
        y = _rmsnorm(y, gf_ref[...])
    o_ref[...] = y


def _const_spec(shape):
    zeros = (0,) * len(shape)
    return pl.BlockSpec(shape, lambda *_: zeros, pipeline_mode=pl.Buffered(1))


def _ffn(x2d, g, wg, wu, wd, gf, *, tm, final_norm):
    n = x2d.shape[0]
    assert n % tm == 0
    row_spec = pl.BlockSpec((tm, D_MODEL), lambda i: (i, 0))
    return pl.pallas_call(
        functools.partial(_ffn_kernel, final_norm=final_norm),
        grid=(n // tm,),
        in_specs=[row_spec, _const_spec(g.shape), _const_spec(wg.shape), _const_spec(wu.shape),
                  _const_spec(wd.shape), _const_spec(gf.shape)],
        out_specs=row_spec,
        out_shape=jax.ShapeDtypeStruct((n, D_MODEL), _F32),
        scratch_shapes=[pltpu.VMEM((tm, D_MODEL), _F32)],
        compiler_params=pltpu.CompilerParams(dimension_semantics=("arbitrary",),
                                             vmem_limit_bytes=VMEM_LIMIT),
        name="ffn_final" if final_norm else "ffn",
    )(x2d, g, wg, wu, wd, gf)


def _safe_scores(q, k, b, blk):
    gs = q.shape[0]
    ri = lax.broadcasted_iota(jnp.int32, (gs, gs), 0)
    ci = lax.broadcasted_iota(jnp.int32, (gs, gs), 1)
    qk = lax.dot_general(q.astype(_BF16), k.astype(_BF16), _NT, preferred_element_type=_F32)
    total = jnp.where(ri == ci, qk, 0.0)
    s = 1
    while s < blk:
        run = ri - ri % (2 * s)
        b_ref_rows = _dot_split_rhs((ci == run + (s - 1)).astype(_BF16), b, 3)
        qs = (q * jnp.exp(jnp.minimum(b - b_ref_rows, 0.0))).astype(_BF16)
        ks = (k * jnp.exp(jnp.minimum(b_ref_rows - b, 0.0))).astype(_BF16)
        level = (ri % (2 * s) >= s) & (ci % (2 * s) < s) & (run == ci - ci % (2 * s))
        total = total + jnp.where(level, lax.dot_general(qs, ks, _NT, preferred_element_type=_F32), 0.0)
        s *= 2
    return total


def _mixer_tile(pcur, st_from, st_to, x_ref, o_ref, mix_ref, gup_ref, gbias_ref, lb_ref, gon_ref, hon_ref, wout_ref,
                *, r0, tm, blk, fill, safe, st_keep=None):
    nb = tm // blk
    rows = pl.ds(r0, tm)
    row = lax.broadcasted_iota(jnp.int32, (tm, tm), 0)
    col = lax.broadcasted_iota(jnp.int32, (tm, tm), 1)
    tri = ((col <= row) & (col >= row - row % blk)).astype(_BF16)
    lane = lax.broadcasted_iota(jnp.int32, (tm, LANES), 1)
    lb = lb_ref[...]

    def cat(x, y, axis):
        return jnp.concatenate([x, y], axis=axis)

    def cat_rows(parts):
        return parts[0] if len(parts) == 1 else jnp.concatenate(parts, axis=0)

    def bdiag(x, y):
        return cat(cat(x, jnp.zeros_like(y), 1), cat(jnp.zeros_like(x), y, 1), 0)

    gd = pcur(_GD, _GD + LANES).astype(_BF16)
    logits = jnp.dot(gd, gup_ref[...], preferred_element_type=_F32) + gbias_ref[...]
    g_cols = [(jnp.minimum(logits, 0.0) - jnp.log1p(jnp.exp(-jnp.abs(logits)))) / GLA_GATE_NORM]
    k_in = []
    for hd in range(HG_HEADS):
        z = pcur(_HF + hd * LANES, _HF + (hd + 1) * LANES)
        lb_h = lb[:, hd * LANES:(hd + 1) * LANES]
        e = jnp.exp(-jnp.abs(z))
        r = 1.0 / (1.0 + e)
        sig_pos = jnp.where(z >= 0, r, e * r)
        sig_neg = jnp.where(z >= 0, e * r, r)
        g_cols.append(jnp.log(lb_h + (1.0 - lb_h) * sig_pos))
        k_in.append((1.0 - lb_h) * sig_neg)
    fill(0)
    b_all = _dot_split_rhs(tri, jnp.concatenate(g_cols, axis=1), 2)
    b_min = b_all[blk - 1:blk, :]
    for c in range(1, nb):
        b_min = jnp.minimum(b_min, b_all[(c + 1) * blk - 1:(c + 1) * blk, :])
    b_min = jnp.min(jnp.broadcast_to(b_min, (8, b_min.shape[1])))

    def k_scale(b):
        if not safe:
            return jnp.exp(-b)
        b_last = [jnp.broadcast_to(b[(c + 1) * blk - 1:(c + 1) * blk, :], (blk, LANES)) for c in range(nb)]
        return jnp.exp(cat_rows(b_last) - b)

    qs, ks, bs, qts, kts, vs, ebs = [], [], [], [], [], [], []
    for p in range(GLA_HEADS // 2):
        b = b_all[:, p * LANES:(p + 1) * LANES]
        eb = jnp.exp(b)
        ek = k_scale(b)
        q = pcur(_GQ + p * LANES, _GQ + (p + 1) * LANES) * (GLA_DK ** -0.5)
        k = pcur(_GK + p * LANES, _GK + (p + 1) * LANES)
        for hh in range(2):
            idx = 2 * p + hh
            m = (lane >= hh * GLA_DK) & (lane < (hh + 1) * GLA_DK)
            qs.append(jnp.where(m, q, 0.0))
            ks.append(jnp.where(m, k, 0.0))
            bs.append(b)
            qts.append((qs[-1] * eb).astype(_BF16))
            kts.append((ks[-1] * ek).astype(_BF16))
            vs.append(pcur(_GV + idx * GLA_DV, _GV + (idx + 1) * GLA_DV).astype(_BF16))
            ebs.append(eb)
    for hd in range(HG_HEADS):
        b = b_all[:, (2 + hd) * LANES:(3 + hd) * LANES]
        eb = jnp.exp(b)
        hq = pcur(_HQ + hd * LANES, _HQ + (hd + 1) * LANES)
        qs.append(hq * _sigmoid(hq))
        ks.append(k_in[hd])
        bs.append(b)
        qts.append((qs[-1] * eb).astype(_BF16))
        kts.append((ks[-1] * k_scale(b)).astype(_BF16))
        vs.append(pcur(_HI + hd * LANES, _HI + (hd + 1) * LANES).astype(_BF16))
        ebs.append(eb)

    gs = min(tm, LANES)
    groups = [slice(u * gs, (u + 1) * gs) for u in range(tm // gs)]
    blocks = [slice(c * blk, (c + 1) * blk) for c in range(nb)]
    pairs = [(2 * p, 2 * p + 1) for p in range(N_HEADS // 2)]
    row2 = lax.broadcasted_iota(jnp.int32, (gs, 2 * gs), 0)
    col2 = lax.broadcasted_iota(jnp.int32, (gs, 2 * gs), 1)
    col2 = jnp.where(col2 >= gs, col2 - gs, col2)
    causal2 = (col2 <= row2) & (col2 >= row2 - row2 % blk)

    q2 = [cat(qts[a], qts[b], 1) for a, b in pairs]
    fill(1)
    kvs = [[lax.dot_general(cat(vs[a][rs], vs[b][rs], 0), bdiag(kts[a][rs], kts[b][rs]), _TN,
                            preferred_element_type=_F32)
            for rs in blocks] for a, b in pairs]

    if safe:
        scores = [[cat(_safe_scores(qs[a][g], ks[a][g], bs[a][g], blk),
                       _safe_scores(qs[b][g], ks[b][g], bs[b][g], blk), 1) for g in groups] for a, b in pairs]
    else:
        scores = [[lax.dot_general(q2[p][g], bdiag(kts[a][g], kts[b][g]), _NT, preferred_element_type=_F32)
                   for g in groups] for p, (a, b) in enumerate(pairs)]
    fill(2)
    o_intra = [[jnp.dot(jnp.where(causal2, scores[p][u], 0.0).astype(_BF16), bdiag(vs[a][g], vs[b][g]),
                        preferred_element_type=_F32)
                for u, g in enumerate(groups)] for p, (a, b) in enumerate(pairs)]
    fill(3)
    states = [cat(st_from[a], st_from[b], 1) for a, b in pairs]
    if st_keep is not None:
        for h in range(N_HEADS):
            st_keep[h] = states[h // 2][:, (h % 2) * LANES:(h % 2 + 1) * LANES]
    e2 = [cat(ebs[a], ebs[b], 1) for a, b in pairs]
    o_inter = [[] for _ in pairs]
    for c, rs in enumerate(blocks):
        for p in range(len(pairs)):
            s2 = states[p].astype(_BF16)
            o_inter[p].append(lax.dot_general(q2[p][rs], bdiag(s2[:, :LANES], s2[:, LANES:]), _NT,
                                              preferred_element_type=_F32))
        for p in range(len(pairs)):
            e_last = e2[p][(c + 1) * blk - 1:(c + 1) * blk, :]
            states[p] = states[p] * e_last + kvs[p][c] if safe else (states[p] + kvs[p][c]) * e_last
    fill(4)
    for p, (a, b) in enumerate(pairs):
        st_to[a] = states[p][:, :LANES]
        st_to[b] = states[p][:, LANES:]
        o2 = cat_rows(o_intra[p]) + cat_rows(o_inter[p])
        for hh, h in enumerate((a, b)):
            o = o2[:, hh * LANES:(hh + 1) * LANES]
            onorm = gon_ref[...] if h < GLA_HEADS else hon_ref[...]
            gate_col = _GR + h * LANES if h < GLA_HEADS else _HGATE + (h - GLA_HEADS) * LANES
            gate = pcur(gate_col, gate_col + LANES)
            o = o * lax.rsqrt(jnp.mean(o * o, axis=-1, keepdims=True) + EPS) * onorm
            o = o * (gate * _sigmoid(gate))
            mix_ref[rows, h * LANES:(h + 1) * LANES] = o.astype(_BF16)
    fill(5)
    out = jnp.dot(mix_ref[rows, :], wout_ref[...], preferred_element_type=_F32)
    fill(6)
    o_ref[rows, :] = x_ref[rows, :] + out
    return b_min


def _mixer_kernel(x_ref, xn_ref, st_in_ref, g_ref, win_ref, gup_ref, gbias_ref, lb_ref, gon_ref, hon_ref, wout_ref,
                  o_ref, st_ref, pa_ref, pb_ref, ha_ref, hb_ref, mix_ref, sk_ref, *, tm, blk, pipelined):
    u = pl.program_id(1)
    tile = functools.partial(_mixer_tile, x_ref=x_ref, o_ref=o_ref, mix_ref=mix_ref, gup_ref=gup_ref,
                             gbias_ref=gbias_ref, lb_ref=lb_ref, gon_ref=gon_ref, hon_ref=hon_ref,
                             wout_ref=wout_ref, tm=tm, blk=blk)
    no_fill = lambda stage: None

    def normed(src_ref, r0):
        return _rmsnorm(src_ref[pl.ds(r0, tm), :], g_ref[...]).astype(_BF16)

    valid = {id(pa_ref): set(range(N_PROJ_PIECES)), id(pb_ref): set()}

    def project(h, dst_ref, pieces, for_reader):
        for i in pieces:
            cols = slice(i * PROJ_PIECE, min((i + 1) * PROJ_PIECE, D_PROJ))
            dst_ref[:, cols] = jnp.dot(h, win_ref[:, cols], preferred_element_type=_F32)
            (valid[id(dst_ref)].add if for_reader else valid[id(dst_ref)].discard)(i)

    def reader(buf_ref):
        def read(c0, c1):
            assert all(i in valid[id(buf_ref)] for i in range(c0 // PROJ_PIECE, (c1 - 1) // PROJ_PIECE + 1))
            return buf_ref[:, c0:c1]
        return read

    def filler(h_ref, dst_ref, for_reader):
        pieces = iter(range(N_PROJ_PIECES))
        return lambda stage: project(h_ref[...], dst_ref, [next(pieces) for _ in range(PROJ_SCHEDULE[stage])],
                                     for_reader)

    @pl.when(u == 0)
    def _():
        for h in range(N_HEADS):
            st_ref[h] = st_in_ref[h].T

    if pipelined:
        @pl.when((pl.program_id(0) == 0) & (u == 0))
        def _():
            project(normed(x_ref, 0), pa_ref, range(N_PROJ_PIECES), True)
            hb_ref[...] = normed(x_ref, tm)

        ha_ref[...] = normed(xn_ref, 0)
        low_a = tile(reader(pa_ref), st_ref, sk_ref.at[1], r0=0, fill=filler(hb_ref, pb_ref, True), safe=False,
                     st_keep=sk_ref.at[0])
        low_b = tile(reader(pb_ref), sk_ref.at[1], st_ref, r0=tm, fill=filler(ha_ref, pa_ref, False), safe=False)
        hb_ref[...] = normed(xn_ref, tm)
        low_ab = jnp.minimum(low_a, low_b)

        def redo(i, carry):
            @pl.when(jnp.where(i == 0, low_a, low_ab) < -DECAY_GUARD)
            def _():
                r0 = pl.multiple_of(i * tm, tm)
                project(normed(x_ref, r0), pb_ref, range(N_PROJ_PIECES), True)
                tile(reader(pb_ref), sk_ref.at[i], sk_ref.at[i + 1], r0=r0, fill=no_fill, safe=True)
            return carry

        lax.fori_loop(0, 2, redo, 0)

        @pl.when(low_ab < -DECAY_GUARD)
        def _():
            for h in range(N_HEADS):
                st_ref[h] = sk_ref[2, h]
    else:
        project(normed(x_ref, 0), pa_ref, range(N_PROJ_PIECES), True)
        low_a = tile(reader(pa_ref), st_ref, st_ref, r0=0, fill=no_fill, safe=False, st_keep=sk_ref.at[0])

        @pl.when(low_a < -DECAY_GUARD)
        def _():
            tile(reader(pa_ref), sk_ref.at[0], st_ref, r0=0, fill=no_fill, safe=True)


    @pl.when(u == pl.num_programs(1) - 1)
    def _():
        for h in range(N_HEADS):
            st_ref[h] = st_ref[h].T


def _mixer(x, st_in, g, win, gup, gbias, lb, gon, hon, wout, *, tm, blk):
    bsz, seq, _ = x.shape
    pipelined = seq % (2 * tm) == 0
    rows = 2 * tm if pipelined else tm
    assert seq % rows == 0 and tm % blk == 0 and blk & (blk - 1) == 0
    nu = seq // rows
    last = bsz * nu - 1

    def next_block(b, u):
        nxt = jnp.minimum(b * nu + u + 1, last)
        return (nxt // nu, nxt % nu, 0)

    row_spec = pl.BlockSpec((None, rows, D_MODEL), lambda b, u: (b, u, 0))
    st_spec = pl.BlockSpec((None, N_HEADS, LANES, LANES), lambda b, u: (b, 0, 0, 0))
    return pl.pallas_call(
        functools.partial(_mixer_kernel, tm=tm, blk=blk, pipelined=pipelined),
        grid=(bsz, nu),
        in_specs=[row_spec, pl.BlockSpec((None, rows, D_MODEL), next_block), st_spec]
        + [_const_spec(a.shape) for a in (g, win, gup, gbias, lb, gon, hon, wout)],
        out_specs=[row_spec, st_spec],
        out_shape=[jax.ShapeDtypeStruct(x.shape, _F32),
                   jax.ShapeDtypeStruct((bsz, N_HEADS, LANES, LANES), _F32)],
        scratch_shapes=[pltpu.VMEM((tm, D_PROJ), _F32), pltpu.VMEM((tm, D_PROJ), _F32),
                        pltpu.VMEM((tm, D_MODEL), _BF16), pltpu.VMEM((tm, D_MODEL), _BF16),
                        pltpu.VMEM((rows, D_MODEL), _BF16),
                        pltpu.VMEM((3, N_HEADS, LANES, LANES), _F32)],
        compiler_params=pltpu.CompilerParams(dimension_semantics=("arbitrary", "arbitrary"),
                                             vmem_limit_bytes=VMEM_LIMIT),
        name="mixer",
    )(x, x, st_in, g, win, gup, gbias, lb, gon, hon, wout)


def _pack_state(s_gla, s_hg):
    zeros = jnp.zeros_like(s_gla)
    even = jnp.concatenate([s_gla, zeros], axis=2)
    odd = jnp.concatenate([zeros, s_gla], axis=2)
    is_odd = (jnp.arange(GLA_HEADS) % 2 == 1)[None, :, None, None]
    gla = jnp.where(is_odd, odd, even)
    return jnp.concatenate([gla, s_hg], axis=1).astype(_F32)


def _unpack_state(st):
    gla = st[:, :GLA_HEADS]
    is_odd = (jnp.arange(GLA_HEADS) % 2 == 1)[None, :, None, None]
    gla = jnp.where(is_odd, gla[:, :, GLA_DK:], gla[:, :, :GLA_DK])
    return gla, st[:, GLA_HEADS:]


def kernel(x_prompt, x_sample, state_gla, state_hgrn, norm_ffn1, ffn1_w_gate, ffn1_w_up, ffn1_w_down, norm_mix, w_in, gla_gate_up, gla_gate_bias, gla_onorm, hg_lower_bound_logits, hg_onorm, w_out, norm_ffn2, ffn2_w_gate, ffn2_w_up, ffn2_w_down, norm_final):
    assert norm_ffn1.shape[0] == 1, "single layer"
    row = lambda v: v.reshape(1, -1).astype(_F32)
    bf = lambda w: w.astype(_BF16)
    f1 = (row(norm_ffn1[0]), bf(ffn1_w_gate[0]), bf(ffn1_w_up[0]), bf(ffn1_w_down[0]), row(norm_final))
    f2 = (row(norm_ffn2[0]), bf(ffn2_w_gate[0]), bf(ffn2_w_up[0]), bf(ffn2_w_down[0]), row(norm_final))
    sizes = (GLA_HEADS * GLA_DK, GLA_HEADS * GLA_DK, GLA_HEADS * GLA_DV, GLA_HEADS * GLA_DV, GLA_GATE_RANK,
             HG_HEADS * HG_EXPAND, HG_HEADS * HG_EXPAND, HG_HEADS * HG_HEAD_DIM, HG_HEADS * HG_HEAD_DIM)
    splits = [sum(sizes[:i + 1]) for i in range(len(sizes) - 1)]
    gq, gk, gv, gr, gd, hq, hf, hi, hgate = jnp.split(w_in[0], splits, axis=1)
    gd = jnp.concatenate([gd, jnp.zeros((D_MODEL, _HF - GLA_GATE_RANK), gd.dtype)], axis=1)
    win = jnp.concatenate([gd, hf, gq, gk, hq, gv, hi, gr, hgate], axis=1).astype(_BF16)
    gup = jnp.concatenate([gla_gate_up[0], jnp.zeros((LANES - GLA_GATE_RANK, GLA_HEADS * GLA_DK), _F32)],
                          axis=0).astype(_BF16)
    lb = jnp.cumsum(jax.nn.softmax(hg_lower_bound_logits.astype(_F32), axis=0), axis=0)[0]
    mix_w = (row(norm_mix[0]), win, gup, row(gla_gate_bias[0]), row(lb), row(gla_onorm[0]), row(hg_onorm[0]),
             bf(w_out[0]))

    def trunk(x, s_gla, s_hg, tm_ffn, tm_mix, blk):
        bsz, seq, _ = x.shape
        x1 = _ffn(x.reshape(bsz * seq, D_MODEL), *f1, tm=tm_ffn, final_norm=False)
        x2, st = _mixer(x1.reshape(bsz, seq, D_MODEL), _pack_state(s_gla, s_hg), *mix_w, tm=tm_mix, blk=blk)
        y = _ffn(x2.reshape(bsz * seq, D_MODEL), *f2, tm=tm_ffn, final_norm=True)
        g, h = _unpack_state(st)
        return y.reshape(bsz, seq, D_MODEL), g[None], h[None]

    bp, sp, _ = x_prompt.shape
    bs, ss, _ = x_sample.shape
    zg = jnp.zeros((bp, GLA_HEADS, GLA_DK, GLA_DV), _F32)
    zh = jnp.zeros((bp, HG_HEADS, HG_EXPAND, HG_HEAD_DIM), _F32)
    blk_p = min(CHUNK, sp)
    y_p, g_p, h_p = trunk(x_prompt, zg, zh, FFN_ROWS, min(MIX_ROWS, sp), blk_p)
    y_s, g_s, h_s = trunk(x_sample, state_gla[0], state_hgrn[0], bs * ss, ss, ss)
    sd_g, sd_h = state_gla.dtype, state_hgrn.dtype
    return (y_p, y_s, g_p.astype(sd_g), h_p.astype(sd_h), g_s.astype(sd_g), h_s.astype(sd_h))
```

```python
import functools

import jax
import jax.numpy as jnp
from jax import lax
from jax.experimental import pallas as pl
from jax.experimental.pallas import tpu as pltpu

D_MODEL = 1024
CHUNK = 64
GLA_HEADS = 4
GLA_DK = 64
GLA_DV = 128
GLA_GATE_RANK = 16
GLA_GATE_NORM = 16.0
HG_HEADS = 4
HG_EXPAND = 128
HG_HEAD_DIM = 128
D_FF = 2816
EPS = 1e-6

LANES = 128
MXU_COLS = 256
N_HEADS = GLA_HEADS + HG_HEADS
FF_CHUNK = MXU_COLS
N_FF_CHUNKS = D_FF // FF_CHUNK
FFN_ROWS = 512
MIX_ROWS = 256
VMEM_LIMIT = 56 * 1024 * 1024
DECAY_GUARD = 60.0

_GD, _HF, _GQ, _GK, _HQ, _GV, _HI, _GR, _HGATE = 0, 256, 768, 1024, 1280, 1792, 2304, 2816, 3328
D_PROJ = _HGATE + HG_HEADS * HG_HEAD_DIM
PROJ_PIECE = 2 * MXU_COLS
N_PROJ_PIECES = -(-D_PROJ // PROJ_PIECE)
PROJ_SCHEDULE = (2, 2, 1, 1, 1, 1, 0)
assert sum(PROJ_SCHEDULE) == N_PROJ_PIECES

_F32 = jnp.float32
_BF16 = jnp.bfloat16


def _rmsnorm(x, g):
    return x * lax.rsqrt(jnp.mean(x * x, axis=-1, keepdims=True) + EPS) * g


def _sigmoid(x):
    return 1.0 / (1.0 + jnp.exp(-x))


def _dot(a, b):
    return jnp.dot(a, b, preferred_element_type=_F32)


_NT = (((1,), (1,)), ((), ()))
_TN = (((0,), (0,)), ((), ()))


def _dot_split_rhs(sel, x, n_parts):
    parts, rest = [], x
    for _ in range(n_parts):
        parts.append(rest.astype(_BF16))
        rest = rest - parts[-1].astype(_F32)
    n = x.shape[1]
    r = jnp.dot(sel, jnp.concatenate(parts, axis=1), preferred_element_type=_F32)
    return sum(r[:, i * n:(i + 1) * n] for i in range(1, n_parts)) + r[:, :n]


def _ffn_kernel(x_ref, g_ref, wg_ref, wu_ref, wd_ref, gf_ref, o_ref, acc_ref, *, final_norm):
    x = x_ref[...]
    h = _rmsnorm(x, g_ref[...]).astype(_BF16)
    for c in range(N_FF_CHUNKS):
        cols = slice(c * FF_CHUNK, (c + 1) * FF_CHUNK)
        gate = _dot(h, wg_ref[:, cols])
        up = _dot(h, wu_ref[:, cols])
        a = (gate * _sigmoid(gate) * up).astype(_BF16)
        d = _dot(a, wd_ref[cols, :])
        if c == 0:
            acc_ref[...] = d
        else:
            acc_ref[...] += d
    y = x + 0.5 * acc_ref[...]
    if final_norm:
        y = _rmsnorm(y, gf_ref[...])
    o_ref[...] = y


def _const_spec(shape):
    zeros = (0,) * len(shape)
    return pl.BlockSpec(shape, lambda *_: zeros, pipeline_mode=pl.Buffered(1))


def _ffn(x2d, g, wg, wu, wd, gf, *, tm, final_norm):
    n = x2d.shape[0]
    assert n % tm == 0
    row_spec = pl.BlockSpec((tm, D_MODEL), lambda i: (i, 0))
    return pl.pallas_call(
        functools.partial(_ffn_kernel, final_norm=final_norm),
        grid=(n // tm,),
        in_specs=[row_spec, _const_spec(g.shape), _const_spec(wg.shape), _const_spec(wu.shape),
                  _const_spec(wd.shape), _const_spec(gf.shape)],
        out_specs=row_spec,
        out_shape=jax.ShapeDtypeStruct((n, D_MODEL), _F32),
        scratch_shapes=[pltpu.VMEM((tm, D_MODEL), _F32)],
        compiler_params=pltpu.CompilerParams(dimension_semantics=("arbitrary",),
                                             vmem_limit_bytes=VMEM_LIMIT),
        name="ffn_final" if final_norm else "ffn",
    )(x2d, g, wg, wu, wd, gf)


def _safe_scores(q, k, b, blk):
    gs = q.shape[0]
    ri = lax.broadcasted_iota(jnp.int32, (gs, gs), 0)
    ci = lax.broadcasted_iota(jnp.int32, (gs, gs), 1)
    qk = lax.dot_general(q.astype(_BF16), k.astype(_BF16), _NT, preferred_element_type=_F32)
    total = jnp.where(ri == ci, qk, 0.0)
    s = 1
    while s < blk:
        run = ri - ri % (2 * s)
        b_ref_rows = _dot_split_rhs((ci == run + (s - 1)).astype(_BF16), b, 3)
        qs = (q * jnp.exp(jnp.minimum(b - b_ref_rows, 0.0))).astype(_BF16)
        ks = (k * jnp.exp(jnp.minimum(b_ref_rows - b, 0.0))).astype(_BF16)
        level = (ri % (2 * s) >= s) & (ci % (2 * s) < s) & (run == ci - ci % (2 * s))
        total = total + jnp.where(level, lax.dot_general(qs, ks, _NT, preferred_element_type=_F32), 0.0)
        s *= 2
    return total


def _mixer_tile(pcur, st_from, st_to, x_ref, o_ref, mix_ref, gup_ref, gbias_ref, lb_ref, gon_ref, hon_ref, wout_ref,
                *, r0, tm, blk, fill, safe, st_keep=None):
    nb = tm // blk
    rows = pl.ds(r0, tm)
    row = lax.broadcasted_iota(jnp.int32, (tm, tm), 0)
    col = lax.broadcasted_iota(jnp.int32, (tm, tm), 1)
    tri = ((col <= row) & (col >= row - row % blk)).astype(_BF16)
    lane = lax.broadcasted_iota(jnp.int32, (tm, LANES), 1)
    lb = lb_ref[...]

    def cat(x, y, axis):
        return jnp.concatenate([x, y], axis=axis)

    def cat_rows(parts):
        return parts[0] if len(parts) == 1 else jnp.concatenate(parts, axis=0)

    def bdiag(x, y):
        return cat(cat(x, jnp.zeros_like(y), 1), cat(jnp.zeros_like(x), y, 1), 0)

    gd = pcur(_GD, _GD + LANES).astype(_BF16)
    logits = jnp.dot(gd, gup_ref[...], preferred_element_type=_F32) + gbias_ref[...]
    g_cols = [(jnp.minimum(logits, 0.0) - jnp.log1p(jnp.exp(-jnp.abs(logits)))) / GLA_GATE_NORM]
    k_in = []
    for hd in range(HG_HEADS):
        z = pcur(_HF + hd * LANES, _HF + (hd + 1) * LANES)
        lb_h = lb[:, hd * LANES:(hd + 1) * LANES]
        e = jnp.exp(-jnp.abs(z))
        r = 1.0 / (1.0 + e)
        sig_pos = jnp.where(z >= 0, r, e * r)
        sig_neg = jnp.where(z >= 0, e * r, r)
        g_cols.append(jnp.log(lb_h + (1.0 - lb_h) * sig_pos))
        k_in.append((1.0 - lb_h) * sig_neg)
    fill(0)
    b_all = _dot_split_rhs(tri, jnp.concatenate(g_cols, axis=1), 2)
    b_min = b_all[blk - 1:blk, :]
    for c in range(1, nb):
        b_min = jnp.minimum(b_min, b_all[(c + 1) * blk - 1:(c + 1) * blk, :])
    b_min = jnp.min(jnp.broadcast_to(b_min, (8, b_min.shape[1])))

    def k_scale(b):
        if not safe:
            return jnp.exp(-b)
        b_last = [jnp.broadcast_to(b[(c + 1) * blk - 1:(c + 1) * blk, :], (blk, LANES)) for c in range(nb)]
        return jnp.exp(cat_rows(b_last) - b)

    qs, ks, bs, qts, kts, vs, ebs = [], [], [], [], [], [], []
    for p in range(GLA_HEADS // 2):
        b = b_all[:, p * LANES:(p + 1) * LANES]
        eb = jnp.exp(b)
        ek = k_scale(b)
        q = pcur(_GQ + p * LANES, _GQ + (p + 1) * LANES) * (GLA_DK ** -0.5)
        k = pcur(_GK + p * LANES, _GK + (p + 1) * LANES)
        for hh in range(2):
            idx = 2 * p + hh
            m = (lane >= hh * GLA_DK) & (lane < (hh + 1) * GLA_DK)
            qs.append(jnp.where(m, q, 0.0))
            ks.append(jnp.where(m, k, 0.0))
            bs.append(b)
            qts.append((qs[-1] * eb).astype(_BF16))
            kts.append((ks[-1] * ek).astype(_BF16))
            vs.append(pcur(_GV + idx * GLA_DV, _GV + (idx + 1) * GLA_DV).astype(_BF16))
            ebs.append(eb)
    for hd in range(HG_HEADS):
        b = b_all[:, (2 + hd) * LANES:(3 + hd) * LANES]
        eb = jnp.exp(b)
        hq = pcur(_HQ + hd * LANES, _HQ + (hd + 1) * LANES)
        qs.append(hq * _sigmoid(hq))
        ks.append(k_in[hd])
        bs.append(b)
        qts.append((qs[-1] * eb).astype(_BF16))
        kts.append((ks[-1] * k_scale(b)).astype(_BF16))
        vs.append(pcur(_HI + hd * LANES, _HI + (hd + 1) * LANES).astype(_BF16))
        ebs.append(eb)

    gs = min(tm, LANES)
    groups = [slice(u * gs, (u + 1) * gs) for u in range(tm // gs)]
    blocks = [slice(c * blk, (c + 1) * blk) for c in range(nb)]
    pairs = [(2 * p, 2 * p + 1) for p in range(N_HEADS // 2)]
    row2 = lax.broadcasted_iota(jnp.int32, (gs, 2 * gs), 0)
    col2 = lax.broadcasted_iota(jnp.int32, (gs, 2 * gs), 1)
    col2 = jnp.where(col2 >= gs, col2 - gs, col2)
    causal2 = (col2 <= row2) & (col2 >= row2 - row2 % blk)

    q2 = [cat(qts[a], qts[b], 1) for a, b in pairs]
    fill(1)
    kvs = [[lax.dot_general(cat(vs[a][rs], vs[b][rs], 0), bdiag(kts[a][rs], kts[b][rs]), _TN,
                            preferred_element_type=_F32)
            for rs in blocks] for a, b in pairs]

    if safe:
        scores = [[cat(_safe_scores(qs[a][g], ks[a][g], bs[a][g], blk),
                       _safe_scores(qs[b][g], ks[b][g], bs[b][g], blk), 1) for g in groups] for a, b in pairs]
    else:
        scores = [[lax.dot_general(q2[p][g], bdiag(kts[a][g], kts[b][g]), _NT, preferred_element_type=_F32)
                   for g in groups] for p, (a, b) in enumerate(pairs)]
    fill(2)
    o_intra = [[jnp.dot(jnp.where(causal2, scores[p][u], 0.0).astype(_BF16), bdiag(vs[a][g], vs[b][g]),
                        preferred_element_type=_F32)
                for u, g in enumerate(groups)] for p, (a, b) in enumerate(pairs)]
    fill(3)
    n_streams = st_from.shape[0] // N_HEADS
    assert n_streams in (1, nb)

    def load_states(stream):
        base = stream * N_HEADS
        loaded = [cat(st_from[base + a], st_from[base + b], 1) for a, b in pairs]
        if st_keep is not None:
            for h in range(N_HEADS):
                st_keep[base + h] = loaded[h // 2][:, (h % 2) * LANES:(h % 2 + 1) * LANES]
        return loaded

    def store_states(stream, values):
        for p, (a, b) in enumerate(pairs):
            st_to[stream * N_HEADS + a] = values[p][:, :LANES]
            st_to[stream * N_HEADS + b] = values[p][:, LANES:]

    states = load_states(0)
    e2 = [cat(ebs[a], ebs[b], 1) for a, b in pairs]
    o_inter = [[] for _ in pairs]
    for c, rs in enumerate(blocks):
        if n_streams > 1 and c > 0:
            states = load_states(c)
        for p in range(len(pairs)):
            s2 = states[p].astype(_BF16)
            o_inter[p].append(lax.dot_general(q2[p][rs], bdiag(s2[:, :LANES], s2[:, LANES:]), _NT,
                                              preferred_element_type=_F32))
        for p in range(len(pairs)):
            e_last = e2[p][(c + 1) * blk - 1:(c + 1) * blk, :]
            states[p] = states[p] * e_last + kvs[p][c] if safe else (states[p] + kvs[p][c]) * e_last
        if n_streams > 1:
            store_states(c, states)
    fill(4)
    if n_streams == 1:
        store_states(0, states)
    for p, (a, b) in enumerate(pairs):
        o2 = cat_rows(o_intra[p]) + cat_rows(o_inter[p])
        for hh, h in enumerate((a, b)):
            o = o2[:, hh * LANES:(hh + 1) * LANES]
            onorm = gon_ref[...] if h < GLA_HEADS else hon_ref[...]
            gate_col = _GR + h * LANES if h < GLA_HEADS else _HGATE + (h - GLA_HEADS) * LANES
            gate = pcur(gate_col, gate_col + LANES)
            o = o * lax.rsqrt(jnp.mean(o * o, axis=-1, keepdims=True) + EPS) * onorm
            o = o * (gate * _sigmoid(gate))
            mix_ref[rows, h * LANES:(h + 1) * LANES] = o.astype(_BF16)
    fill(5)
    out = jnp.dot(mix_ref[rows, :], wout_ref[...], preferred_element_type=_F32)
    fill(6)
    o_ref[rows, :] = x_ref[rows, :] + out
    return b_min


def _mixer_kernel(x_ref, xn_ref, st_in_ref, g_ref, win_ref, gup_ref, gbias_ref, lb_ref, gon_ref, hon_ref, wout_ref,
                  o_ref, st_ref, pa_ref, pb_ref, ha_ref, hb_ref, mix_ref, sk_ref, *, tm, blk, pipelined):
    u = pl.program_id(1)
    tile = functools.partial(_mixer_tile, x_ref=x_ref, o_ref=o_ref, mix_ref=mix_ref, gup_ref=gup_ref,
                             gbias_ref=gbias_ref, lb_ref=lb_ref, gon_ref=gon_ref, hon_ref=hon_ref,
                             wout_ref=wout_ref, tm=tm, blk=blk)
    no_fill = lambda stage: None

    def normed(src_ref, r0):
        return _rmsnorm(src_ref[pl.ds(r0, tm), :], g_ref[...]).astype(_BF16)

    valid = {id(pa_ref): set(range(N_PROJ_PIECES)), id(pb_ref): set()}

    def project(h, dst_ref, pieces, for_reader):
        for i in pieces:
            cols = slice(i * PROJ_PIECE, min((i + 1) * PROJ_PIECE, D_PROJ))
            dst_ref[:, cols] = jnp.dot(h, win_ref[:, cols], preferred_element_type=_F32)
            (valid[id(dst_ref)].add if for_reader else valid[id(dst_ref)].discard)(i)

    def reader(buf_ref):
        def read(c0, c1):
            assert all(i in valid[id(buf_ref)] for i in range(c0 // PROJ_PIECE, (c1 - 1) // PROJ_PIECE + 1))
            return buf_ref[:, c0:c1]
        return read

    def filler(h_ref, dst_ref, for_reader):
        pieces = iter(range(N_PROJ_PIECES))
        return lambda stage: project(h_ref[...], dst_ref, [next(pieces) for _ in range(PROJ_SCHEDULE[stage])],
                                     for_reader)

    @pl.when(u == 0)
    def _():
        for h in range(st_ref.shape[0]):
            st_ref[h] = st_in_ref[h].T

    if pipelined:
        @pl.when((pl.program_id(0) == 0) & (u == 0))
        def _():
            project(normed(x_ref, 0), pa_ref, range(N_PROJ_PIECES), True)
            hb_ref[...] = normed(x_ref, tm)

        ha_ref[...] = normed(xn_ref, 0)
        low_a = tile(reader(pa_ref), st_ref, sk_ref.at[1], r0=0, fill=filler(hb_ref, pb_ref, True), safe=False,
                     st_keep=sk_ref.at[0])
        low_b = tile(reader(pb_ref), sk_ref.at[1], st_ref, r0=tm, fill=filler(ha_ref, pa_ref, False), safe=False)
        hb_ref[...] = normed(xn_ref, tm)
        low_ab = jnp.minimum(low_a, low_b)

        def redo(i, carry):
            @pl.when(jnp.where(i == 0, low_a, low_ab) < -DECAY_GUARD)
            def _():
                r0 = pl.multiple_of(i * tm, tm)
                project(normed(x_ref, r0), pb_ref, range(N_PROJ_PIECES), True)
                tile(reader(pb_ref), sk_ref.at[i], sk_ref.at[i + 1], r0=r0, fill=no_fill, safe=True)
            return carry

        lax.fori_loop(0, 2, redo, 0)

        @pl.when(low_ab < -DECAY_GUARD)
        def _():
            for h in range(N_HEADS):
                st_ref[h] = sk_ref[2, h]
    else:
        project(normed(x_ref, 0), pa_ref, range(N_PROJ_PIECES), True)
        low_a = tile(reader(pa_ref), st_ref, st_ref, r0=0, fill=no_fill, safe=False, st_keep=sk_ref.at[0])

        @pl.when(low_a < -DECAY_GUARD)
        def _():
            tile(reader(pa_ref), sk_ref.at[0], st_ref, r0=0, fill=no_fill, safe=True)


    @pl.when(u == pl.num_programs(1) - 1)
    def _():
        for h in range(st_ref.shape[0]):
            st_ref[h] = st_ref[h].T


def _mixer(x, st_in, g, win, gup, gbias, lb, gon, hon, wout, *, tm, blk):
    bsz, seq, _ = x.shape
    pipelined = seq % (2 * tm) == 0
    rows = 2 * tm if pipelined else tm
    assert seq % rows == 0 and tm % blk == 0 and blk & (blk - 1) == 0
    nu = seq // rows
    last = bsz * nu - 1

    def next_block(b, u):
        nxt = jnp.minimum(b * nu + u + 1, last)
        return (nxt // nu, nxt % nu, 0)

    row_spec = pl.BlockSpec((None, rows, D_MODEL), lambda b, u: (b, u, 0))
    n_states = st_in.shape[1]
    assert n_states == N_HEADS or (not pipelined and n_states == (tm // blk) * N_HEADS)
    st_spec = pl.BlockSpec((None, n_states, LANES, LANES), lambda b, u: (b, 0, 0, 0))
    return pl.pallas_call(
        functools.partial(_mixer_kernel, tm=tm, blk=blk, pipelined=pipelined),
        grid=(bsz, nu),
        in_specs=[row_spec, pl.BlockSpec((None, rows, D_MODEL), next_block), st_spec]
        + [_const_spec(a.shape) for a in (g, win, gup, gbias, lb, gon, hon, wout)],
        out_specs=[row_spec, st_spec],
        out_shape=[jax.ShapeDtypeStruct(x.shape, _F32),
                   jax.ShapeDtypeStruct((bsz, n_states, LANES, LANES), _F32)],
        scratch_shapes=[pltpu.VMEM((tm, D_PROJ), _F32), pltpu.VMEM((tm, D_PROJ), _F32),
                        pltpu.VMEM((tm, D_MODEL), _BF16), pltpu.VMEM((tm, D_MODEL), _BF16),
                        pltpu.VMEM((rows, D_MODEL), _BF16),
                        pltpu.VMEM((3 if pipelined else 1, n_states, LANES, LANES), _F32)],
        compiler_params=pltpu.CompilerParams(dimension_semantics=("arbitrary", "arbitrary"),
                                             vmem_limit_bytes=VMEM_LIMIT),
        name="mixer",
    )(x, x, st_in, g, win, gup, gbias, lb, gon, hon, wout)


def _pack_state(s_gla, s_hg):
    zeros = jnp.zeros_like(s_gla)
    even = jnp.concatenate([s_gla, zeros], axis=2)
    odd = jnp.concatenate([zeros, s_gla], axis=2)
    is_odd = (jnp.arange(GLA_HEADS) % 2 == 1)[None, :, None, None]
    gla = jnp.where(is_odd, odd, even)
    return jnp.concatenate([gla, s_hg], axis=1).astype(_F32)


def _unpack_state(st):
    gla = st[:, :GLA_HEADS]
    is_odd = (jnp.arange(GLA_HEADS) % 2 == 1)[None, :, None, None]
    gla = jnp.where(is_odd, gla[:, :, GLA_DK:], gla[:, :, :GLA_DK])
    return gla, st[:, GLA_HEADS:]


def kernel(x_prompt, x_sample, state_gla, state_hgrn, norm_ffn1, ffn1_w_gate, ffn1_w_up, ffn1_w_down, norm_mix, w_in, gla_gate_up, gla_gate_bias, gla_onorm, hg_lower_bound_logits, hg_onorm, w_out, norm_ffn2, ffn2_w_gate, ffn2_w_up, ffn2_w_down, norm_final):
    assert norm_ffn1.shape[0] == 1, "single layer"
    row = lambda v: v.reshape(1, -1).astype(_F32)
    bf = lambda w: w.astype(_BF16)
    f1 = (row(norm_ffn1[0]), bf(ffn1_w_gate[0]), bf(ffn1_w_up[0]), bf(ffn1_w_down[0]), row(norm_final))
    f2 = (row(norm_ffn2[0]), bf(ffn2_w_gate[0]), bf(ffn2_w_up[0]), bf(ffn2_w_down[0]), row(norm_final))
    sizes = (GLA_HEADS * GLA_DK, GLA_HEADS * GLA_DK, GLA_HEADS * GLA_DV, GLA_HEADS * GLA_DV, GLA_GATE_RANK,
             HG_HEADS * HG_EXPAND, HG_HEADS * HG_EXPAND, HG_HEADS * HG_HEAD_DIM, HG_HEADS * HG_HEAD_DIM)
    splits = [sum(sizes[:i + 1]) for i in range(len(sizes) - 1)]
    gq, gk, gv, gr, gd, hq, hf, hi, hgate = jnp.split(w_in[0], splits, axis=1)
    gd = jnp.concatenate([gd, jnp.zeros((D_MODEL, _HF - GLA_GATE_RANK), gd.dtype)], axis=1)
    win = jnp.concatenate([gd, hf, gq, gk, hq, gv, hi, gr, hgate], axis=1).astype(_BF16)
    gup = jnp.concatenate([gla_gate_up[0], jnp.zeros((LANES - GLA_GATE_RANK, GLA_HEADS * GLA_DK), _F32)],
                          axis=0).astype(_BF16)
    lb = jnp.cumsum(jax.nn.softmax(hg_lower_bound_logits.astype(_F32), axis=0), axis=0)[0]
    mix_w = (row(norm_mix[0]), win, gup, row(gla_gate_bias[0]), row(lb), row(gla_onorm[0]), row(hg_onorm[0]),
             bf(w_out[0]))

    def trunk(x, s_gla, s_hg, tm_ffn, tm_mix, blk):
        bsz, seq, _ = x.shape
        x1 = _ffn(x.reshape(bsz * seq, D_MODEL), *f1, tm=tm_ffn, final_norm=False)
        st = _pack_state(s_gla, s_hg)
        if seq == blk and (bsz * seq) % LANES == 0 and bsz * seq <= tm_mix:
            x2, st = _mixer(x1.reshape(1, bsz * seq, D_MODEL), st.reshape(1, bsz * N_HEADS, LANES, LANES), *mix_w,
                            tm=bsz * seq, blk=blk)
            st = st.reshape(bsz, N_HEADS, LANES, LANES)
        else:
            x2, st = _mixer(x1.reshape(bsz, seq, D_MODEL), st, *mix_w, tm=min(tm_mix, seq), blk=blk)
        y = _ffn(x2.reshape(bsz * seq, D_MODEL), *f2, tm=tm_ffn, final_norm=True)
        g, h = _unpack_state(st)
        return y.reshape(bsz, seq, D_MODEL), g[None], h[None]

    bp, sp, _ = x_prompt.shape
    bs, ss, _ = x_sample.shape
    zg = jnp.zeros((bp, GLA_HEADS, GLA_DK, GLA_DV), _F32)
    zh = jnp.zeros((bp, HG_HEADS, HG_EXPAND, HG_HEAD_DIM), _F32)
    blk_p = min(CHUNK, sp)
    y_p, g_p, h_p = trunk(x_prompt, zg, zh, FFN_ROWS, MIX_ROWS, blk_p)
    y_s, g_s, h_s = trunk(x_sample, state_gla[0], state_hgrn[0], bs * ss, MIX_ROWS, ss)
    sd_g, sd_h = state_gla.dtype, state_hgrn.dtype
    return (y_p, y_s, g_p.astype(sd_g), h_p.astype(sd_h), g_s.astype(sd_g), h_s.astype(sd_h))
```

```python
import functools

import jax
import jax.numpy as jnp
from jax import lax
from jax.experimental import pallas as pl
from jax.experimental.pallas import tpu as pltpu

D_MODEL = 1024
CHUNK = 64
GLA_HEADS = 4
GLA_DK = 64
GLA_DV = 128
GLA_GATE_RANK = 16
GLA_GATE_NORM = 16.0
HG_HEADS = 4
HG_EXPAND = 128
HG_HEAD_DIM = 128
D_FF = 2816
EPS = 1e-6

LANES = 128
MXU_COLS = 256
N_HEADS = GLA_HEADS + HG_HEADS
FF_CHUNK = MXU_COLS
N_FF_CHUNKS = D_FF // FF_CHUNK
FFN_ROWS = 512
MIX_ROWS = 256
VMEM_LIMIT = 56 * 1024 * 1024
DECAY_GUARD = 60.0

_GD, _HF, _GQ, _GK, _HQ, _GV, _HI, _GR, _HGATE = 0, 256, 768, 1024, 1280, 1792, 2304, 2816, 3328
D_PROJ = _HGATE + HG_HEADS * HG_HEAD_DIM
PROJ_PIECE = 2 * MXU_COLS
N_PROJ_PIECES = -(-D_PROJ // PROJ_PIECE)
PROJ_SCHEDULE = (2, 2, 1, 1, 1, 1, 0)
assert sum(PROJ_SCHEDULE) == N_PROJ_PIECES

_F32 = jnp.float32
_BF16 = jnp.bfloat16


def _rmsnorm(x, g):
    return x * lax.rsqrt(jnp.mean(x * x, axis=-1, keepdims=True) + EPS) * g


def _sigmoid(x):
    return 1.0 / (1.0 + jnp.exp(-x))


def _dot(a, b):
    return jnp.dot(a, b, preferred_element_type=_F32)


_NT = (((1,), (1,)), ((), ()))
_TN = (((0,), (0,)), ((), ()))


def _dot_split_rhs(sel, x, n_parts):
    parts, rest = [], x
    for _ in range(n_parts):
        parts.append(rest.astype(_BF16))
        rest = rest - parts[-1].astype(_F32)
    n = x.shape[1]
    r = jnp.dot(sel, jnp.concatenate(parts, axis=1), preferred_element_type=_F32)
    return sum(r[:, i * n:(i + 1) * n] for i in range(1, n_parts)) + r[:, :n]


def _ffn_kernel(x_ref, g_ref, wg_ref, wu_ref, wd_ref, gf_ref, o_ref, acc_ref, *, final_norm):
    x = x_ref[...]
    h = _rmsnorm(x, g_ref[...]).astype(_BF16)
    for c in range(N_FF_CHUNKS):
        cols = slice(c * FF_CHUNK, (c + 1) * FF_CHUNK)
        gate = _dot(h, wg_ref[:, cols].astype(_BF16))
        up = _dot(h, wu_ref[:, cols].astype(_BF16))
        a = (gate * _sigmoid(gate) * up).astype(_BF16)
        d = _dot(a, wd_ref[cols, :].astype(_BF16))
        if c == 0:
            acc_ref[...] = d
        else:
            acc_ref[...] += d
    y = x + 0.5 * acc_ref[...]
    if final_norm:
        y = _rmsnorm(y, gf_ref[...])
    o_ref[...] = y


def _const_spec(shape):
    zeros = (0,) * len(shape)
    return pl.BlockSpec(shape, lambda *_: zeros, pipeline_mode=pl.Buffered(1))


def _ffn(x2d, g, wg, wu, wd, gf, *, tm, final_norm):
    n = x2d.shape[0]
    assert n % tm == 0
    row_spec = pl.BlockSpec((tm, D_MODEL), lambda i: (i, 0))
    return pl.pallas_call(
        functools.partial(_ffn_kernel, final_norm=final_norm),
        grid=(n // tm,),
        in_specs=[row_spec, _const_spec(g.shape), _const_spec(wg.shape), _const_spec(wu.shape),
                  _const_spec(wd.shape), _const_spec(gf.shape)],
        out_specs=row_spec,
        out_shape=jax.ShapeDtypeStruct((n, D_MODEL), _F32),
        scratch_shapes=[pltpu.VMEM((tm, D_MODEL), _F32)],
        compiler_params=pltpu.CompilerParams(dimension_semantics=("arbitrary",),
                                             vmem_limit_bytes=VMEM_LIMIT),
        name="ffn_final" if final_norm else "ffn",
    )(x2d, g, wg, wu, wd, gf)


def _safe_scores(q, k, b, blk):
    gs = q.shape[0]
    ri = lax.broadcasted_iota(jnp.int32, (gs, gs), 0)
    ci = lax.broadcasted_iota(jnp.int32, (gs, gs), 1)
    qk = lax.dot_general(q.astype(_BF16), k.astype(_BF16), _NT, preferred_element_type=_F32)
    total = jnp.where(ri == ci, qk, 0.0)
    s = 1
    while s < blk:
        run = ri - ri % (2 * s)
        b_ref_rows = _dot_split_rhs((ci == run + (s - 1)).astype(_BF16), b, 3)
        qs = (q * jnp.exp(jnp.minimum(b - b_ref_rows, 0.0))).astype(_BF16)
        ks = (k * jnp.exp(jnp.minimum(b_ref_rows - b, 0.0))).astype(_BF16)
        level = (ri % (2 * s) >= s) & (ci % (2 * s) < s) & (run == ci - ci % (2 * s))
        total = total + jnp.where(level, lax.dot_general(qs, ks, _NT, preferred_element_type=_F32), 0.0)
        s *= 2
    return total


def _mixer_tile(pcur, st_from, st_to, x_ref, o_ref, mix_ref, gup_ref, gbias_ref, lb_ref, gon_ref, hon_ref, wout_ref,
                *, r0, tm, blk, fill, safe, st_keep=None):
    nb = tm // blk
    rows = pl.ds(r0, tm)
    row = lax.broadcasted_iota(jnp.int32, (tm, tm), 0)
    col = lax.broadcasted_iota(jnp.int32, (tm, tm), 1)
    tri = ((col <= row) & (col >= row - row % blk)).astype(_BF16)
    lane = lax.broadcasted_iota(jnp.int32, (tm, LANES), 1)
    lb = lb_ref[...]

    def cat(x, y, axis):
        return jnp.concatenate([x, y], axis=axis)

    def cat_rows(parts):
        return parts[0] if len(parts) == 1 else jnp.concatenate(parts, axis=0)

    def bdiag(x, y):
        return cat(cat(x, jnp.zeros_like(y), 1), cat(jnp.zeros_like(x), y, 1), 0)

    gd = pcur(_GD, _GD + LANES).astype(_BF16)
    logits = jnp.dot(gd, gup_ref[...], preferred_element_type=_F32) + gbias_ref[...]
    g_cols = [(jnp.minimum(logits, 0.0) - jnp.log1p(jnp.exp(-jnp.abs(logits)))) / GLA_GATE_NORM]
    k_in = []
    for hd in range(HG_HEADS):
        z = pcur(_HF + hd * LANES, _HF + (hd + 1) * LANES)
        lb_h = lb[:, hd * LANES:(hd + 1) * LANES]
        e = jnp.exp(-jnp.abs(z))
        r = 1.0 / (1.0 + e)
        sig_pos = jnp.where(z >= 0, r, e * r)
        sig_neg = jnp.where(z >= 0, e * r, r)
        g_cols.append(jnp.log(lb_h + (1.0 - lb_h) * sig_pos))
        k_in.append((1.0 - lb_h) * sig_neg)
    fill(0)
    b_all = _dot_split_rhs(tri, jnp.concatenate(g_cols, axis=1), 2)
    b_min = b_all[blk - 1:blk, :]
    for c in range(1, nb):
        b_min = jnp.minimum(b_min, b_all[(c + 1) * blk - 1:(c + 1) * blk, :])
    b_min = jnp.min(jnp.broadcast_to(b_min, (8, b_min.shape[1])))

    def k_scale(b):
        if not safe:
            return jnp.exp(-b)
        b_last = [jnp.broadcast_to(b[(c + 1) * blk - 1:(c + 1) * blk, :], (blk, LANES)) for c in range(nb)]
        return jnp.exp(cat_rows(b_last) - b)

    qs, ks, bs, qts, kts, vs, ebs = [], [], [], [], [], [], []
    for p in range(GLA_HEADS // 2):
        b = b_all[:, p * LANES:(p + 1) * LANES]
        eb = jnp.exp(b)
        ek = k_scale(b)
        q = pcur(_GQ + p * LANES, _GQ + (p + 1) * LANES) * (GLA_DK ** -0.5)
        k = pcur(_GK + p * LANES, _GK + (p + 1) * LANES)
        for hh in range(2):
            idx = 2 * p + hh
            m = (lane >= hh * GLA_DK) & (lane < (hh + 1) * GLA_DK)
            qs.append(jnp.where(m, q, 0.0))
            ks.append(jnp.where(m, k, 0.0))
            bs.append(b)
            qts.append((qs[-1] * eb).astype(_BF16))
            kts.append((ks[-1] * ek).astype(_BF16))
            vs.append(pcur(_GV + idx * GLA_DV, _GV + (idx + 1) * GLA_DV).astype(_BF16))
            ebs.append(eb)
    for hd in range(HG_HEADS):
        b = b_all[:, (2 + hd) * LANES:(3 + hd) * LANES]
        eb = jnp.exp(b)
        hq = pcur(_HQ + hd * LANES, _HQ + (hd + 1) * LANES)
        qs.append(hq * _sigmoid(hq))
        ks.append(k_in[hd])
        bs.append(b)
        qts.append((qs[-1] * eb).astype(_BF16))
        kts.append((ks[-1] * k_scale(b)).astype(_BF16))
        vs.append(pcur(_HI + hd * LANES, _HI + (hd + 1) * LANES).astype(_BF16))
        ebs.append(eb)

    gs = min(tm, LANES)
    groups = [slice(u * gs, (u + 1) * gs) for u in range(tm // gs)]
    blocks = [slice(c * blk, (c + 1) * blk) for c in range(nb)]
    pairs = [(2 * p, 2 * p + 1) for p in range(N_HEADS // 2)]
    row2 = lax.broadcasted_iota(jnp.int32, (gs, 2 * gs), 0)
    col2 = lax.broadcasted_iota(jnp.int32, (gs, 2 * gs), 1)
    col2 = jnp.where(col2 >= gs, col2 - gs, col2)
    causal2 = (col2 <= row2) & (col2 >= row2 - row2 % blk)

    q2 = [cat(qts[a], qts[b], 1) for a, b in pairs]
    fill(1)
    kvs = [[lax.dot_general(cat(vs[a][rs], vs[b][rs], 0), bdiag(kts[a][rs], kts[b][rs]), _TN,
                            preferred_element_type=_F32)
            for rs in blocks] for a, b in pairs]

    if safe:
        scores = [[cat(_safe_scores(qs[a][g], ks[a][g], bs[a][g], blk),
                       _safe_scores(qs[b][g], ks[b][g], bs[b][g], blk), 1) for g in groups] for a, b in pairs]
    else:
        scores = [[lax.dot_general(q2[p][g], bdiag(kts[a][g], kts[b][g]), _NT, preferred_element_type=_F32)
                   for g in groups] for p, (a, b) in enumerate(pairs)]
    fill(2)
    o_intra = [[jnp.dot(jnp.where(causal2, scores[p][u], 0.0).astype(_BF16), bdiag(vs[a][g], vs[b][g]),
                        preferred_element_type=_F32)
                for u, g in enumerate(groups)] for p, (a, b) in enumerate(pairs)]
    fill(3)
    n_streams = st_from.shape[0] // N_HEADS
    assert n_streams in (1, nb)

    def load_states(stream):
        base = stream * N_HEADS
        loaded = [cat(st_from[base + a], st_from[base + b], 1) for a, b in pairs]
        if st_keep is not None:
            for h in range(N_HEADS):
                st_keep[base + h] = loaded[h // 2][:, (h % 2) * LANES:(h % 2 + 1) * LANES]
        return loaded

    def store_states(stream, values):
        for p, (a, b) in enumerate(pairs):
            st_to[stream * N_HEADS + a] = values[p][:, :LANES]
            st_to[stream * N_HEADS + b] = values[p][:, LANES:]

    states = load_states(0)
    e2 = [cat(ebs[a], ebs[b], 1) for a, b in pairs]
    o_inter = [[] for _ in pairs]
    for c, rs in enumerate(blocks):
        if n_streams > 1 and c > 0:
            states = load_states(c)
        for p in range(len(pairs)):
            s2 = states[p].astype(_BF16)
            o_inter[p].append(lax.dot_general(q2[p][rs], bdiag(s2[:, :LANES], s2[:, LANES:]), _NT,
                                              preferred_element_type=_F32))
        for p in range(len(pairs)):
            e_last = e2[p][(c + 1) * blk - 1:(c + 1) * blk, :]
            states[p] = states[p] * e_last + kvs[p][c] if safe else (states[p] + kvs[p][c]) * e_last
        if n_streams > 1:
            store_states(c, states)
    fill(4)
    if n_streams == 1:
        store_states(0, states)
    for p, (a, b) in enumerate(pairs):
        o2 = cat_rows(o_intra[p]) + cat_rows(o_inter[p])
        for hh, h in enumerate((a, b)):
            o = o2[:, hh * LANES:(hh + 1) * LANES]
            onorm = gon_ref[...] if h < GLA_HEADS else hon_ref[...]
            gate_col = _GR + h * LANES if h < GLA_HEADS else _HGATE + (h - GLA_HEADS) * LANES
            gate = pcur(gate_col, gate_col + LANES)
            o = o * lax.rsqrt(jnp.mean(o * o, axis=-1, keepdims=True) + EPS) * onorm
            o = o * (gate * _sigmoid(gate))
            mix_ref[rows, h * LANES:(h + 1) * LANES] = o.astype(_BF16)
    fill(5)
    out = jnp.dot(mix_ref[rows, :], wout_ref[...], preferred_element_type=_F32)
    fill(6)
    o_ref[rows, :] = x_ref[rows, :] + out
    return b_min


def _mixer_kernel(x_ref, xn_ref, st_in_ref, g_ref, win_ref, gup_ref, gbias_ref, lb_ref, gon_ref, hon_ref, wout_ref,
                  o_ref, st_ref, pa_ref, pb_ref, ha_ref, hb_ref, mix_ref, sk_ref, *, tm, blk, pipelined):
    u = pl.program_id(1)
    tile = functools.partial(_mixer_tile, x_ref=x_ref, o_ref=o_ref, mix_ref=mix_ref, gup_ref=gup_ref,
                             gbias_ref=gbias_ref, lb_ref=lb_ref, gon_ref=gon_ref, hon_ref=hon_ref,
                             wout_ref=wout_ref, tm=tm, blk=blk)
    no_fill = lambda stage: None

    def normed(src_ref, r0):
        return _rmsnorm(src_ref[pl.ds(r0, tm), :], g_ref[...]).astype(_BF16)

    valid = {id(pa_ref): set(range(N_PROJ_PIECES)), id(pb_ref): set()}

    def project(h, dst_ref, pieces, for_reader):
        for i in pieces:
            cols = slice(i * PROJ_PIECE, min((i + 1) * PROJ_PIECE, D_PROJ))
            dst_ref[:, cols] = jnp.dot(h, win_ref[:, cols], preferred_element_type=_F32)
            (valid[id(dst_ref)].add if for_reader else valid[id(dst_ref)].discard)(i)

    def reader(buf_ref):
        def read(c0, c1):
            assert all(i in valid[id(buf_ref)] for i in range(c0 // PROJ_PIECE, (c1 - 1) // PROJ_PIECE + 1))
            return buf_ref[:, c0:c1]
        return read

    def filler(h_ref, dst_ref, for_reader):
        pieces = iter(range(N_PROJ_PIECES))
        return lambda stage: project(h_ref[...], dst_ref, [next(pieces) for _ in range(PROJ_SCHEDULE[stage])],
                                     for_reader)

    @pl.when(u == 0)
    def _():
        for h in range(st_ref.shape[0]):
            st_ref[h] = st_in_ref[h].T

    if pipelined:
        @pl.when((pl.program_id(0) == 0) & (u == 0))
        def _():
            project(normed(x_ref, 0), pa_ref, range(N_PROJ_PIECES), True)
            hb_ref[...] = normed(x_ref, tm)

        ha_ref[...] = normed(xn_ref, 0)
        low_a = tile(reader(pa_ref), st_ref, sk_ref.at[1], r0=0, fill=filler(hb_ref, pb_ref, True), safe=False,
                     st_keep=sk_ref.at[0])
        low_b = tile(reader(pb_ref), sk_ref.at[1], st_ref, r0=tm, fill=filler(ha_ref, pa_ref, False), safe=False)
        hb_ref[...] = normed(xn_ref, tm)
        low_ab = jnp.minimum(low_a, low_b)

        def redo(i, carry):
            @pl.when(jnp.where(i == 0, low_a, low_ab) < -DECAY_GUARD)
            def _():
                r0 = pl.multiple_of(i * tm, tm)
                project(normed(x_ref, r0), pb_ref, range(N_PROJ_PIECES), True)
                tile(reader(pb_ref), sk_ref.at[i], sk_ref.at[i + 1], r0=r0, fill=no_fill, safe=True)
            return carry

        lax.fori_loop(0, 2, redo, 0)

        @pl.when(low_ab < -DECAY_GUARD)
        def _():
            for h in range(N_HEADS):
                st_ref[h] = sk_ref[2, h]
    else:
        project(normed(x_ref, 0), pa_ref, range(N_PROJ_PIECES), True)
        low_a = tile(reader(pa_ref), st_ref, st_ref, r0=0, fill=no_fill, safe=False, st_keep=sk_ref.at[0])

        @pl.when(low_a < -DECAY_GUARD)
        def _():
            tile(reader(pa_ref), sk_ref.at[0], st_ref, r0=0, fill=no_fill, safe=True)


    @pl.when(u == pl.num_programs(1) - 1)
    def _():
        for h in range(st_ref.shape[0]):
            st_ref[h] = st_ref[h].T


def _mixer(x, st_in, g, win, gup, gbias, lb, gon, hon, wout, *, tm, blk):
    bsz, seq, _ = x.shape
    pipelined = seq % (2 * tm) == 0
    rows = 2 * tm if pipelined else tm
    assert seq % rows == 0 and tm % blk == 0 and blk & (blk - 1) == 0
    nu = seq // rows
    last = bsz * nu - 1

    def next_block(b, u):
        nxt = jnp.minimum(b * nu + u + 1, last)
        return (nxt // nu, nxt % nu, 0)

    row_spec = pl.BlockSpec((None, rows, D_MODEL), lambda b, u: (b, u, 0))
    n_states = st_in.shape[1]
    assert n_states == N_HEADS or (not pipelined and n_states == (tm // blk) * N_HEADS)
    st_spec = pl.BlockSpec((None, n_states, LANES, LANES), lambda b, u: (b, 0, 0, 0))
    return pl.pallas_call(
        functools.partial(_mixer_kernel, tm=tm, blk=blk, pipelined=pipelined),
        grid=(bsz, nu),
        in_specs=[row_spec, pl.BlockSpec((None, rows, D_MODEL), next_block), st_spec]
        + [_const_spec(a.shape) for a in (g, win, gup, gbias, lb, gon, hon, wout)],
        out_specs=[row_spec, st_spec],
        out_shape=[jax.ShapeDtypeStruct(x.shape, _F32),
                   jax.ShapeDtypeStruct((bsz, n_states, LANES, LANES), _F32)],
        scratch_shapes=[pltpu.VMEM((tm, D_PROJ), _F32), pltpu.VMEM((tm, D_PROJ), _F32),
                        pltpu.VMEM((tm, D_MODEL), _BF16), pltpu.VMEM((tm, D_MODEL), _BF16),
                        pltpu.VMEM((rows, D_MODEL), _BF16),
                        pltpu.VMEM((3 if pipelined else 1, n_states, LANES, LANES), _F32)],
        compiler_params=pltpu.CompilerParams(dimension_semantics=("arbitrary", "arbitrary"),
                                             vmem_limit_bytes=VMEM_LIMIT),
        name="mixer",
    )(x, x, st_in, g, win, gup, gbias, lb, gon, hon, wout)


def _pack_state(s_gla, s_hg):
    zeros = jnp.zeros_like(s_gla)
    even = jnp.concatenate([s_gla, zeros], axis=2)
    odd = jnp.concatenate([zeros, s_gla], axis=2)
    is_odd = (jnp.arange(GLA_HEADS) % 2 == 1)[None, :, None, None]
    gla = jnp.where(is_odd, odd, even)
    return jnp.concatenate([gla, s_hg], axis=1).astype(_F32)


def _unpack_state(st):
    gla = st[:, :GLA_HEADS]
    is_odd = (jnp.arange(GLA_HEADS) % 2 == 1)[None, :, None, None]
    gla = jnp.where(is_odd, gla[:, :, GLA_DK:], gla[:, :, :GLA_DK])
    return gla, st[:, GLA_HEADS:]


def kernel(x_prompt, x_sample, state_gla, state_hgrn, norm_ffn1, ffn1_w_gate, ffn1_w_up, ffn1_w_down, norm_mix, w_in, gla_gate_up, gla_gate_bias, gla_onorm, hg_lower_bound_logits, hg_onorm, w_out, norm_ffn2, ffn2_w_gate, ffn2_w_up, ffn2_w_down, norm_final):
    assert norm_ffn1.shape[0] == 1, "single layer"
    row = lambda v: v.reshape(1, -1).astype(_F32)
    bf = lambda w: w.astype(_BF16)
    f1 = (row(norm_ffn1[0]), ffn1_w_gate[0], ffn1_w_up[0], ffn1_w_down[0], row(norm_final))
    f2 = (row(norm_ffn2[0]), ffn2_w_gate[0], ffn2_w_up[0], ffn2_w_down[0], row(norm_final))
    sizes = (GLA_HEADS * GLA_DK, GLA_HEADS * GLA_DK, GLA_HEADS * GLA_DV, GLA_HEADS * GLA_DV, GLA_GATE_RANK,
             HG_HEADS * HG_EXPAND, HG_HEADS * HG_EXPAND, HG_HEADS * HG_HEAD_DIM, HG_HEADS * HG_HEAD_DIM)
    splits = [sum(sizes[:i + 1]) for i in range(len(sizes) - 1)]
    gq, gk, gv, gr, gd, hq, hf, hi, hgate = jnp.split(w_in[0], splits, axis=1)
    gd = jnp.concatenate([gd, jnp.zeros((D_MODEL, _HF - GLA_GATE_RANK), gd.dtype)], axis=1)
    win = jnp.concatenate([gd, hf, gq, gk, hq, gv, hi, gr, hgate], axis=1).astype(_BF16)
    gup = jnp.concatenate([gla_gate_up[0], jnp.zeros((LANES - GLA_GATE_RANK, GLA_HEADS * GLA_DK), _F32)],
                          axis=0).astype(_BF16)
    lb = jnp.cumsum(jax.nn.softmax(hg_lower_bound_logits.astype(_F32), axis=0), axis=0)[0]
    mix_w = (row(norm_mix[0]), win, gup, row(gla_gate_bias[0]), row(lb), row(gla_onorm[0]), row(hg_onorm[0]),
             bf(w_out[0]))

    def trunk(x, s_gla, s_hg, tm_ffn, tm_mix, blk):
        bsz, seq, _ = x.shape
        x1 = _ffn(x.reshape(bsz * seq, D_MODEL), *f1, tm=tm_ffn, final_norm=False)
        st = _pack_state(s_gla, s_hg)
        if seq == blk and (bsz * seq) % LANES == 0 and bsz * seq <= tm_mix:
            x2, st = _mixer(x1.reshape(1, bsz * seq, D_MODEL), st.reshape(1, bsz * N_HEADS, LANES, LANES), *mix_w,
                            tm=bsz * seq, blk=blk)
            st = st.reshape(bsz, N_HEADS, LANES, LANES)
        else:
            x2, st = _mixer(x1.reshape(bsz, seq, D_MODEL), st, *mix_w, tm=min(tm_mix, seq), blk=blk)
        y = _ffn(x2.reshape(bsz * seq, D_MODEL), *f2, tm=tm_ffn, final_norm=True)
        g, h = _unpack_state(st)
        return y.reshape(bsz, seq, D_MODEL), g[None], h[None]

    bp, sp, _ = x_prompt.shape
    bs, ss, _ = x_sample.shape
    zg = jnp.zeros((bp, GLA_HEADS, GLA_DK, GLA_DV), _F32)
    zh = jnp.zeros((bp, HG_HEADS, HG_EXPAND, HG_HEAD_DIM), _F32)
    blk_p = min(CHUNK, sp)
    y_p, g_p, h_p = trunk(x_prompt, zg, zh, FFN_ROWS, MIX_ROWS, blk_p)
    y_s, g_s, h_s = trunk(x_sample, state_gla[0], state_hgrn[0], bs * ss, MIX_ROWS, ss)
    sd_g, sd_h = state_gla.dtype, state_hgrn.dtype
    return (y_p, y_s, g_p.astype(sd_g), h_p.astype(sd_h), g_s.astype(sd_g), h_s.astype(sd_h))
```

```python
import functools

import jax
import jax.numpy as jnp
from jax import lax
from jax.experimental import pallas as pl
from jax.experimental.pallas import tpu as pltpu

D_MODEL = 1024
CHUNK = 64
GLA_HEADS = 4
GLA_DK = 64
GLA_DV = 128
GLA_GATE_RANK = 16
GLA_GATE_NORM = 16.0
HG_HEADS = 4
HG_EXPAND = 128
HG_HEAD_DIM = 128
D_FF = 2816
EPS = 1e-6

LANES = 128
MXU_COLS = 256
N_HEADS = GLA_HEADS + HG_HEADS
FF_CHUNK = MXU_COLS
N_FF_CHUNKS = D_FF // FF_CHUNK
FFN_ROWS = 512
MIX_ROWS = 256
VMEM_LIMIT = 56 * 1024 * 1024
DECAY_GUARD = 60.0

_GD, _HF, _GQ, _GK, _HQ, _GV, _HI, _GR, _HGATE = 0, 256, 768, 1024, 1280, 1792, 2304, 2816, 3328
D_PROJ = _HGATE + HG_HEADS * HG_HEAD_DIM
PROJ_PIECE = 2 * MXU_COLS
N_PROJ_PIECES = -(-D_PROJ // PROJ_PIECE)
PROJ_SCHEDULE = (2, 2, 1, 1, 1, 1, 0)
assert sum(PROJ_SCHEDULE) == N_PROJ_PIECES

_F32 = jnp.float32
_BF16 = jnp.bfloat16


def _rmsnorm(x, g):
    return x * lax.rsqrt(jnp.mean(x * x, axis=-1, keepdims=True) + EPS) * g


def _sigmoid(x):
    return 1.0 / (1.0 + jnp.exp(-x))


def _dot(a, b):
    return jnp.dot(a, b, preferred_element_type=_F32)


_NT = (((1,), (1,)), ((), ()))
_TN = (((0,), (0,)), ((), ()))


def _dot_split_rhs(sel, x, n_parts):
    parts, rest = [], x
    for _ in range(n_parts):
        parts.append(rest.astype(_BF16))
        rest = rest - parts[-1].astype(_F32)
    n = x.shape[1]
    r = jnp.dot(sel, jnp.concatenate(parts, axis=1), preferred_element_type=_F32)
    return sum(r[:, i * n:(i + 1) * n] for i in range(1, n_parts)) + r[:, :n]


def _ffn_kernel(x_ref, g_ref, wg_ref, wu_ref, wd_ref, gf_ref, o_ref, acc_ref, *, final_norm):
    x = x_ref[...]
    h = _rmsnorm(x, g_ref[...]).astype(_BF16)
    for c in range(N_FF_CHUNKS):
        cols = slice(c * FF_CHUNK, (c + 1) * FF_CHUNK)
        gate = _dot(h, wg_ref[:, cols].astype(_BF16))
        up = _dot(h, wu_ref[:, cols].astype(_BF16))
        a = (gate * _sigmoid(gate) * up).astype(_BF16)
        d = _dot(a, wd_ref[cols, :].astype(_BF16))
        if c == 0:
            acc_ref[...] = d
        else:
            acc_ref[...] += d
    y = x + 0.5 * acc_ref[...]
    if final_norm:
        y = _rmsnorm(y, gf_ref[...])
    o_ref[...] = y


def _const_spec(shape):
    zeros = (0,) * len(shape)
    return pl.BlockSpec(shape, lambda *_: zeros, pipeline_mode=pl.Buffered(1))


def _ffn(x2d, g, wg, wu, wd, gf, *, tm, final_norm):
    n = x2d.shape[0]
    assert n % tm == 0
    row_spec = pl.BlockSpec((tm, D_MODEL), lambda i: (i, 0))
    return pl.pallas_call(
        functools.partial(_ffn_kernel, final_norm=final_norm),
        grid=(n // tm,),
        in_specs=[row_spec, _const_spec(g.shape), _const_spec(wg.shape), _const_spec(wu.shape),
                  _const_spec(wd.shape), _const_spec(gf.shape)],
        out_specs=row_spec,
        out_shape=jax.ShapeDtypeStruct((n, D_MODEL), _F32),
        scratch_shapes=[pltpu.VMEM((tm, D_MODEL), _F32)],
        compiler_params=pltpu.CompilerParams(dimension_semantics=("arbitrary",),
                                             vmem_limit_bytes=VMEM_LIMIT),
        name="ffn_final" if final_norm else "ffn",
    )(x2d, g, wg, wu, wd, gf)


def _safe_scores(q, k, b, blk):
    gs = q.shape[0]
    ri = lax.broadcasted_iota(jnp.int32, (gs, gs), 0)
    ci = lax.broadcasted_iota(jnp.int32, (gs, gs), 1)
    qk = lax.dot_general(q.astype(_BF16), k.astype(_BF16), _NT, preferred_element_type=_F32)
    total = jnp.where(ri == ci, qk, 0.0)
    s = 1
    while s < blk:
        run = ri - ri % (2 * s)
        b_ref_rows = _dot_split_rhs((ci == run + (s - 1)).astype(_BF16), b, 3)
        qs = (q * jnp.exp(jnp.minimum(b - b_ref_rows, 0.0))).astype(_BF16)
        ks = (k * jnp.exp(jnp.minimum(b_ref_rows - b, 0.0))).astype(_BF16)
        level = (ri % (2 * s) >= s) & (ci % (2 * s) < s) & (run == ci - ci % (2 * s))
        total = total + jnp.where(level, lax.dot_general(qs, ks, _NT, preferred_element_type=_F32), 0.0)
        s *= 2
    return total


def _mixer_tile(pcur, st_from, st_to, x_ref, o_ref, mix_ref, gup_ref, gbias_ref, lb_ref, gon_ref, hon_ref, wout_ref,
                *, r0, tm, blk, fill, safe, st_keep=None):
    nb = tm // blk
    rows = pl.ds(r0, tm)
    row = lax.broadcasted_iota(jnp.int32, (tm, tm), 0)
    col = lax.broadcasted_iota(jnp.int32, (tm, tm), 1)
    tri = ((col <= row) & (col >= row - row % blk)).astype(_BF16)
    lane = lax.broadcasted_iota(jnp.int32, (tm, LANES), 1)
    lb = lb_ref[...]

    def cat(x, y, axis):
        return jnp.concatenate([x, y], axis=axis)

    def cat_rows(parts):
        return parts[0] if len(parts) == 1 else jnp.concatenate(parts, axis=0)

    def bdiag(x, y):
        return cat(cat(x, jnp.zeros_like(y), 1), cat(jnp.zeros_like(x), y, 1), 0)

    gd = pcur(_GD, _GD + LANES).astype(_BF16)
    logits = jnp.dot(gd, gup_ref[...], preferred_element_type=_F32) + gbias_ref[...]
    g_cols = [(jnp.minimum(logits, 0.0) - jnp.log1p(jnp.exp(-jnp.abs(logits)))) / GLA_GATE_NORM]
    k_in = []
    for hd in range(HG_HEADS):
        z = pcur(_HF + hd * LANES, _HF + (hd + 1) * LANES)
        lb_h = lb[:, hd * LANES:(hd + 1) * LANES]
        e = jnp.exp(-jnp.abs(z))
        r = 1.0 / (1.0 + e)
        sig_pos = jnp.where(z >= 0, r, e * r)
        sig_neg = jnp.where(z >= 0, e * r, r)
        g_cols.append(jnp.log(lb_h + (1.0 - lb_h) * sig_pos))
        k_in.append((1.0 - lb_h) * sig_neg)
    fill(0)
    b_all = _dot_split_rhs(tri, jnp.concatenate(g_cols, axis=1), 2)
    b_min = b_all[blk - 1:blk, :]
    for c in range(1, nb):
        b_min = jnp.minimum(b_min, b_all[(c + 1) * blk - 1:(c + 1) * blk, :])
    b_min = jnp.min(jnp.broadcast_to(b_min, (8, b_min.shape[1])))

    def k_scale(b):
        if not safe:
            return jnp.exp(-b)
        b_last = [jnp.broadcast_to(b[(c + 1) * blk - 1:(c + 1) * blk, :], (blk, LANES)) for c in range(nb)]
        return jnp.exp(cat_rows(b_last) - b)

    qs, ks, bs, qts, kts, vs, ebs = [], [], [], [], [], [], []
    for p in range(GLA_HEADS // 2):
        b = b_all[:, p * LANES:(p + 1) * LANES]
        eb = jnp.exp(b)
        ek = k_scale(b)
        q = pcur(_GQ + p * LANES, _GQ + (p + 1) * LANES) * (GLA_DK ** -0.5)
        k = pcur(_GK + p * LANES, _GK + (p + 1) * LANES)
        for hh in range(2):
            idx = 2 * p + hh
            m = (lane >= hh * GLA_DK) & (lane < (hh + 1) * GLA_DK)
            qs.append(jnp.where(m, q, 0.0))
            ks.append(jnp.where(m, k, 0.0))
            bs.append(b)
            qts.append((qs[-1] * eb).astype(_BF16))
            kts.append((ks[-1] * ek).astype(_BF16))
            vs.append(pcur(_GV + idx * GLA_DV, _GV + (idx + 1) * GLA_DV).astype(_BF16))
            ebs.append(eb)
    for hd in range(HG_HEADS):
        b = b_all[:, (2 + hd) * LANES:(3 + hd) * LANES]
        eb = jnp.exp(b)
        hq = pcur(_HQ + hd * LANES, _HQ + (hd + 1) * LANES)
        qs.append(hq * _sigmoid(hq))
        ks.append(k_in[hd])
        bs.append(b)
        qts.append((qs[-1] * eb).astype(_BF16))
        kts.append((ks[-1] * k_scale(b)).astype(_BF16))
        vs.append(pcur(_HI + hd * LANES, _HI + (hd + 1) * LANES).astype(_BF16))
        ebs.append(eb)

    gs = min(tm, LANES)
    groups = [slice(u * gs, (u + 1) * gs) for u in range(tm // gs)]
    blocks = [slice(c * blk, (c + 1) * blk) for c in range(nb)]
    pairs = [(2 * p, 2 * p + 1) for p in range(N_HEADS // 2)]
    row2 = lax.broadcasted_iota(jnp.int32, (gs, 2 * gs), 0)
    col2 = lax.broadcasted_iota(jnp.int32, (gs, 2 * gs), 1)
    col2 = jnp.where(col2 >= gs, col2 - gs, col2)
    causal2 = (col2 <= row2) & (col2 >= row2 - row2 % blk)

    q2 = [cat(qts[a], qts[b], 1) for a, b in pairs]
    fill(1)
    kvs = [[lax.dot_general(cat(vs[a][rs], vs[b][rs], 0), bdiag(kts[a][rs], kts[b][rs]), _TN,
                            preferred_element_type=_F32)
            for rs in blocks] for a, b in pairs]

    if safe:
        scores = [[cat(_safe_scores(qs[a][g], ks[a][g], bs[a][g], blk),
                       _safe_scores(qs[b][g], ks[b][g], bs[b][g], blk), 1) for g in groups] for a, b in pairs]
    else:
        scores = [[lax.dot_general(q2[p][g], bdiag(kts[a][g], kts[b][g]), _NT, preferred_element_type=_F32)
                   for g in groups] for p, (a, b) in enumerate(pairs)]
    fill(2)
    o_intra = [[jnp.dot(jnp.where(causal2, scores[p][u], 0.0).astype(_BF16), bdiag(vs[a][g], vs[b][g]),
                        preferred_element_type=_F32)
                for u, g in enumerate(groups)] for p, (a, b) in enumerate(pairs)]
    fill(3)
    n_streams = st_from.shape[0] // N_HEADS
    assert n_streams in (1, nb)

    def load_states(stream):
        base = stream * N_HEADS
        loaded = [cat(st_from[base + a], st_from[base + b], 1) for a, b in pairs]
        if st_keep is not None:
            for h in range(N_HEADS):
                st_keep[base + h] = loaded[h // 2][:, (h % 2) * LANES:(h % 2 + 1) * LANES]
        return loaded

    def store_states(stream, values):
        for p, (a, b) in enumerate(pairs):
            st_to[stream * N_HEADS + a] = values[p][:, :LANES]
            st_to[stream * N_HEADS + b] = values[p][:, LANES:]

    states = load_states(0)
    e2 = [cat(ebs[a], ebs[b], 1) for a, b in pairs]
    o_inter = [[] for _ in pairs]
    for c, rs in enumerate(blocks):
        if n_streams > 1 and c > 0:
            states = load_states(c)
        for p in range(len(pairs)):
            s2 = states[p].astype(_BF16)
            o_inter[p].append(lax.dot_general(q2[p][rs], bdiag(s2[:, :LANES], s2[:, LANES:]), _NT,
                                              preferred_element_type=_F32))
        for p in range(len(pairs)):
            e_last = e2[p][(c + 1) * blk - 1:(c + 1) * blk, :]
            states[p] = states[p] * e_last + kvs[p][c] if safe else (states[p] + kvs[p][c]) * e_last
        if n_streams > 1:
            store_states(c, states)
    fill(4)
    if n_streams == 1:
        store_states(0, states)
    for p, (a, b) in enumerate(pairs):
        o2 = cat_rows(o_intra[p]) + cat_rows(o_inter[p])
        for hh, h in enumerate((a, b)):
            o = o2[:, hh * LANES:(hh + 1) * LANES]
            onorm = gon_ref[...] if h < GLA_HEADS else hon_ref[...]
            gate_col = _GR + h * LANES if h < GLA_HEADS else _HGATE + (h - GLA_HEADS) * LANES
            gate = pcur(gate_col, gate_col + LANES)
            o = o * lax.rsqrt(jnp.mean(o * o, axis=-1, keepdims=True) + EPS) * onorm
            o = o * (gate * _sigmoid(gate))
            mix_ref[rows, h * LANES:(h + 1) * LANES] = o.astype(_BF16)
    fill(5)
    out = jnp.dot(mix_ref[rows, :], wout_ref[...], preferred_element_type=_F32)
    fill(6)
    o_ref[rows, :] = x_ref[rows, :] + out
    return b_min


def _mixer_kernel(x_ref, xn_ref, st_in_ref, g_ref, win_ref, gup_ref, gbias_ref, lb_ref, gon_ref, hon_ref, wout_ref,
                  o_ref, st_ref, pa_ref, pb_ref, ha_ref, hb_ref, mix_ref, sk_ref, *, tm, blk, pipelined):
    u = pl.program_id(1)
    tile = functools.partial(_mixer_tile, x_ref=x_ref, o_ref=o_ref, mix_ref=mix_ref, gup_ref=gup_ref,
                             gbias_ref=gbias_ref, lb_ref=lb_ref, gon_ref=gon_ref, hon_ref=hon_ref,
                             wout_ref=wout_ref, tm=tm, blk=blk)
    no_fill = lambda stage: None

    def normed(src_ref, r0):
        return _rmsnorm(src_ref[pl.ds(r0, tm), :], g_ref[...]).astype(_BF16)

    valid = {id(pa_ref): set(range(N_PROJ_PIECES)), id(pb_ref): set()}

    def project(h, dst_ref, pieces, for_reader):
        for i in pieces:
            cols = slice(i * PROJ_PIECE, min((i + 1) * PROJ_PIECE, D_PROJ))
            dst_ref[:, cols] = jnp.dot(h, win_ref[:, cols], preferred_element_type=_F32)
            (valid[id(dst_ref)].add if for_reader else valid[id(dst_ref)].discard)(i)

    def reader(buf_ref):
        def read(c0, c1):
            assert all(i in valid[id(buf_ref)] for i in range(c0 // PROJ_PIECE, (c1 - 1) // PROJ_PIECE + 1))
            return buf_ref[:, c0:c1]
        return read

    def filler(h_ref, dst_ref, for_reader):
        pieces = iter(range(N_PROJ_PIECES))
        return lambda stage: project(h_ref[...], dst_ref, [next(pieces) for _ in range(PROJ_SCHEDULE[stage])],
                                     for_reader)

    @pl.when(u == 0)
    def _():
        for h in range(st_ref.shape[0]):
            st_ref[h] = st_in_ref[h].T

    if pipelined:
        @pl.when((pl.program_id(0) == 0) & (u == 0))
        def _():
            project(normed(x_ref, 0), pa_ref, range(N_PROJ_PIECES), True)
            hb_ref[...] = normed(x_ref, tm)

        ha_ref[...] = normed(xn_ref, 0)
        low_a = tile(reader(pa_ref), st_ref, sk_ref.at[1], r0=0, fill=filler(hb_ref, pb_ref, True), safe=False,
                     st_keep=sk_ref.at[0])
        low_b = tile(reader(pb_ref), sk_ref.at[1], st_ref, r0=tm, fill=filler(ha_ref, pa_ref, False), safe=False)
        hb_ref[...] = normed(xn_ref, tm)
        low_ab = jnp.minimum(low_a, low_b)

        def redo(i, carry):
            @pl.when(jnp.where(i == 0, low_a, low_ab) < -DECAY_GUARD)
            def _():
                r0 = pl.multiple_of(i * tm, tm)
                project(normed(x_ref, r0), pb_ref, range(N_PROJ_PIECES), True)
                tile(reader(pb_ref), sk_ref.at[i], sk_ref.at[i + 1], r0=r0, fill=no_fill, safe=True)
            return carry

        lax.fori_loop(0, 2, redo, 0)

        @pl.when(low_ab < -DECAY_GUARD)
        def _():
            for h in range(N_HEADS):
                st_ref[h] = sk_ref[2, h]
    else:
        project(normed(x_ref, 0), pa_ref, range(N_PROJ_PIECES), True)
        low_a = tile(reader(pa_ref), st_ref, st_ref, r0=0, fill=no_fill, safe=False, st_keep=sk_ref.at[0])

        @pl.when(low_a < -DECAY_GUARD)
        def _():
            tile(reader(pa_ref), sk_ref.at[0], st_ref, r0=0, fill=no_fill, safe=True)


    @pl.when(u == pl.num_programs(1) - 1)
    def _():
        for h in range(st_ref.shape[0]):
            st_ref[h] = st_ref[h].T


def _mixer(x, st_in, g, win, gup, gbias, lb, gon, hon, wout, *, tm, blk):
    bsz, seq, _ = x.shape
    pipelined = seq % (2 * tm) == 0
    rows = 2 * tm if pipelined else tm
    assert seq % rows == 0 and tm % blk == 0 and blk & (blk - 1) == 0
    nu = seq // rows
    last = bsz * nu - 1

    def next_block(b, u):
        nxt = jnp.minimum(b * nu + u + 1, last)
        return (nxt // nu, nxt % nu, 0)

    row_spec = pl.BlockSpec((None, rows, D_MODEL), lambda b, u: (b, u, 0))
    n_states = st_in.shape[1]
    assert n_states == N_HEADS or (not pipelined and n_states == (tm // blk) * N_HEADS)
    st_spec = pl.BlockSpec((None, n_states, LANES, LANES), lambda b, u: (b, 0, 0, 0))
    return pl.pallas_call(
        functools.partial(_mixer_kernel, tm=tm, blk=blk, pipelined=pipelined),
        grid=(bsz, nu),
        in_specs=[row_spec, pl.BlockSpec((None, rows, D_MODEL), next_block), st_spec]
        + [_const_spec(a.shape) for a in (g, win, gup, gbias, lb, gon, hon, wout)],
        out_specs=[row_spec, st_spec],
        out_shape=[jax.ShapeDtypeStruct(x.shape, _F32),
                   jax.ShapeDtypeStruct((bsz, n_states, LANES, LANES), _F32)],
        scratch_shapes=[pltpu.VMEM((tm, D_PROJ), _F32), pltpu.VMEM((tm, D_PROJ), _F32),
                        pltpu.VMEM((tm, D_MODEL), _BF16), pltpu.VMEM((tm, D_MODEL), _BF16),
                        pltpu.VMEM((rows, D_MODEL), _BF16),
                        pltpu.VMEM((3 if pipelined else 1, n_states, LANES, LANES), _F32)],
        compiler_params=pltpu.CompilerParams(dimension_semantics=("arbitrary", "arbitrary"),
                                             vmem_limit_bytes=VMEM_LIMIT),
        name="mixer",
    )(x, x, st_in, g, win, gup, gbias, lb, gon, hon, wout)


_W_IN_GROUPS = (("gq", GLA_HEADS * GLA_DK, _GQ), ("gk", GLA_HEADS * GLA_DK, _GK), ("gv", GLA_HEADS * GLA_DV, _GV),
                ("gr", GLA_HEADS * GLA_DV, _GR), ("gd", GLA_GATE_RANK, _GD), ("hq", HG_HEADS * HG_EXPAND, _HQ),
                ("hf", HG_HEADS * HG_EXPAND, _HF), ("hi", HG_HEADS * HG_HEAD_DIM, _HI),
                ("hgate", HG_HEADS * HG_HEAD_DIM, _HGATE))
W_IN_ROWS = 128


def _w_in_kernel(w_ref, o_ref):
    o_ref[:, _GD:_HF] = jnp.zeros((W_IN_ROWS, _HF - _GD), _BF16)
    src = 0
    for _, width, dst in _W_IN_GROUPS:
        o_ref[:, dst:dst + width] = w_ref[:, src:src + width].astype(_BF16)
        src += width


def _reorder_w_in(w):
    d_in = sum(width for _, width, _ in _W_IN_GROUPS)
    assert w.shape == (D_MODEL, d_in)
    return pl.pallas_call(
        _w_in_kernel,
        grid=(D_MODEL // W_IN_ROWS,),
        in_specs=[pl.BlockSpec((W_IN_ROWS, d_in), lambda i: (i, 0))],
        out_specs=pl.BlockSpec((W_IN_ROWS, D_PROJ), lambda i: (i, 0)),
        out_shape=jax.ShapeDtypeStruct((D_MODEL, D_PROJ), _BF16),
        name="w_in_order",
    )(w)


def _pack_state(s_gla, s_hg):
    zeros = jnp.zeros_like(s_gla)
    even = jnp.concatenate([s_gla, zeros], axis=2)
    odd = jnp.concatenate([zeros, s_gla], axis=2)
    is_odd = (jnp.arange(GLA_HEADS) % 2 == 1)[None, :, None, None]
    gla = jnp.where(is_odd, odd, even)
    return jnp.concatenate([gla, s_hg], axis=1).astype(_F32)


def _unpack_state(st):
    gla = st[:, :GLA_HEADS]
    is_odd = (jnp.arange(GLA_HEADS) % 2 == 1)[None, :, None, None]
    gla = jnp.where(is_odd, gla[:, :, GLA_DK:], gla[:, :, :GLA_DK])
    return gla, st[:, GLA_HEADS:]


def kernel(x_prompt, x_sample, state_gla, state_hgrn, norm_ffn1, ffn1_w_gate, ffn1_w_up, ffn1_w_down, norm_mix, w_in, gla_gate_up, gla_gate_bias, gla_onorm, hg_lower_bound_logits, hg_onorm, w_out, norm_ffn2, ffn2_w_gate, ffn2_w_up, ffn2_w_down, norm_final):
    assert norm_ffn1.shape[0] == 1, "single layer"
    row = lambda v: v.reshape(1, -1).astype(_F32)
    bf = lambda w: w.astype(_BF16)
    f1 = (row(norm_ffn1[0]), ffn1_w_gate[0], ffn1_w_up[0], ffn1_w_down[0], row(norm_final))
    f2 = (row(norm_ffn2[0]), ffn2_w_gate[0], ffn2_w_up[0], ffn2_w_down[0], row(norm_final))
    win = _reorder_w_in(w_in[0])
    gup = jnp.concatenate([gla_gate_up[0], jnp.zeros((LANES - GLA_GATE_RANK, GLA_HEADS * GLA_DK), _F32)],
                          axis=0).astype(_BF16)
    lb = jnp.cumsum(jax.nn.softmax(hg_lower_bound_logits.astype(_F32), axis=0), axis=0)[0]
    mix_w = (row(norm_mix[0]), win, gup, row(gla_gate_bias[0]), row(lb), row(gla_onorm[0]), row(hg_onorm[0]),
             bf(w_out[0]))

    def trunk(x, s_gla, s_hg, tm_ffn, tm_mix, blk):
        bsz, seq, _ = x.shape
        x1 = _ffn(x.reshape(bsz * seq, D_MODEL), *f1, tm=tm_ffn, final_norm=False)
        st = _pack_state(s_gla, s_hg)
        if seq == blk and (bsz * seq) % LANES == 0 and bsz * seq <= tm_mix:
            x2, st = _mixer(x1.reshape(1, bsz * seq, D_MODEL), st.reshape(1, bsz * N_HEADS, LANES, LANES), *mix_w,
                            tm=bsz * seq, blk=blk)
            st = st.reshape(bsz, N_HEADS, LANES, LANES)
        else:
            x2, st = _mixer(x1.reshape(bsz, seq, D_MODEL), st, *mix_w, tm=min(tm_mix, seq), blk=blk)
        y = _ffn(x2.reshape(bsz * seq, D_MODEL), *f2, tm=tm_ffn, final_norm=True)
        g, h = _unpack_state(st)
        return y.reshape(bsz, seq, D_MODEL), g[None], h[None]

    bp, sp, _ = x_prompt.shape
    bs, ss, _ = x_sample.shape
    zg = jnp.zeros((bp, GLA_HEADS, GLA_DK, GLA_DV), _F32)
    zh = jnp.zeros((bp, HG_HEADS, HG_EXPAND, HG_HEAD_DIM), _F32)
    blk_p = min(CHUNK, sp)
    y_p, g_p, h_p = trunk(x_prompt, zg, zh, FFN_ROWS, MIX_ROWS, blk_p)
    y_s, g_s, h_s = trunk(x_sample, state_gla[0], state_hgrn[0], bs * ss, MIX_ROWS, ss)
    sd_g, sd_h = state_gla.dtype, state_hgrn.dtype
    return (y_p, y_s, g_p.astype(sd_g), h_p.astype(sd_h), g_s.astype(sd_g), h_s.astype(sd_h))
```

```python
import functools

import jax
import jax.numpy as jnp
from jax import lax
from jax.experimental import pallas as pl
from jax.experimental.pallas import tpu as pltpu

D_MODEL = 1024
CHUNK = 64
GLA_HEADS = 4
GLA_DK = 64
GLA_DV = 128
GLA_GATE_RANK = 16
GLA_GATE_NORM = 16.0
HG_HEADS = 4
HG_EXPAND = 128
HG_HEAD_DIM = 128
D_FF = 2816
EPS = 1e-6

LANES = 128
SUBLANES = 8
MXU_COLS = 256
N_HEADS = GLA_HEADS + HG_HEADS
FF_CHUNK = MXU_COLS
N_FF_CHUNKS = D_FF // FF_CHUNK
FFN_ROWS = 512
MIX_ROWS = 256
VMEM_LIMIT = 56 * 1024 * 1024
DECAY_GUARD = 60.0

_GD, _HF, _GQ, _GK, _HQ, _GV, _HI, _GR, _HGATE = 0, 256, 768, 1024, 1280, 1792, 2304, 2816, 3328
D_PROJ = _HGATE + HG_HEADS * HG_HEAD_DIM
PROJ_PIECE = 2 * MXU_COLS
N_PROJ_PIECES = -(-D_PROJ // PROJ_PIECE)
PROJ_SCHEDULE = (2, 2, 1, 1, 1, 1, 0)
assert sum(PROJ_SCHEDULE) == N_PROJ_PIECES

_F32 = jnp.float32
_BF16 = jnp.bfloat16


def _rmsnorm(x, g):
    return x * lax.rsqrt(jnp.mean(x * x, axis=-1, keepdims=True) + EPS) * g


def _sigmoid(x):
    return 1.0 / (1.0 + jnp.exp(-x))


def _dot(a, b):
    return jnp.dot(a, b, preferred_element_type=_F32)


_NT = (((1,), (1,)), ((), ()))
_TN = (((0,), (0,)), ((), ()))


def _dot_split_rhs(sel, x, n_parts):
    parts, rest = [], x
    for _ in range(n_parts):
        parts.append(rest.astype(_BF16))
        rest = rest - parts[-1].astype(_F32)
    n = x.shape[1]
    r = jnp.dot(sel, jnp.concatenate(parts, axis=1), preferred_element_type=_F32)
    return sum(r[:, i * n:(i + 1) * n] for i in range(1, n_parts)) + r[:, :n]


def _ffn_kernel(x_ref, g_ref, wg_ref, wu_ref, wd_ref, gf_ref, o_ref, acc_ref, *, final_norm):
    x = x_ref[...]
    h = _rmsnorm(x, g_ref[...]).astype(_BF16)
    for c in range(N_FF_CHUNKS):
        cols = slice(c * FF_CHUNK, (c + 1) * FF_CHUNK)
        gate = _dot(h, wg_ref[:, cols].astype(_BF16))
        up = _dot(h, wu_ref[:, cols].astype(_BF16))
        a = (gate * _sigmoid(gate) * up).astype(_BF16)
        d = _dot(a, wd_ref[cols, :].astype(_BF16))
        if c == 0:
            acc_ref[...] = d
        else:
            acc_ref[...] += d
    y = x + 0.5 * acc_ref[...]
    if final_norm:
        y = _rmsnorm(y, gf_ref[...])
    o_ref[...] = y


def _const_spec(shape):
    zeros = (0,) * len(shape)
    return pl.BlockSpec(shape, lambda *_: zeros, pipeline_mode=pl.Buffered(1))


def _ffn(x2d, g, wg, wu, wd, gf, *, tm, final_norm):
    n = x2d.shape[0]
    assert n % tm == 0
    row_spec = pl.BlockSpec((tm, D_MODEL), lambda i: (i, 0))
    return pl.pallas_call(
        functools.partial(_ffn_kernel, final_norm=final_norm),
        grid=(n // tm,),
        in_specs=[row_spec, _const_spec(g.shape), _const_spec(wg.shape), _const_spec(wu.shape),
                  _const_spec(wd.shape), _const_spec(gf.shape)],
        out_specs=row_spec,
        out_shape=jax.ShapeDtypeStruct((n, D_MODEL), _F32),
        scratch_shapes=[pltpu.VMEM((tm, D_MODEL), _F32)],
        compiler_params=pltpu.CompilerParams(dimension_semantics=("arbitrary",),
                                             vmem_limit_bytes=VMEM_LIMIT),
        name="ffn_final" if final_norm else "ffn",
    )(x2d, g, wg, wu, wd, gf)


def _safe_scores(q, k, b, blk):
    gs = q.shape[0]
    ri = lax.broadcasted_iota(jnp.int32, (gs, gs), 0)
    ci = lax.broadcasted_iota(jnp.int32, (gs, gs), 1)
    qk = lax.dot_general(q.astype(_BF16), k.astype(_BF16), _NT, preferred_element_type=_F32)
    total = jnp.where(ri == ci, qk, 0.0)
    s = 1
    while s < blk:
        run = ri - ri % (2 * s)
        b_ref_rows = _dot_split_rhs((ci == run + (s - 1)).astype(_BF16), b, 3)
        qs = (q * jnp.exp(jnp.minimum(b - b_ref_rows, 0.0))).astype(_BF16)
        ks = (k * jnp.exp(jnp.minimum(b_ref_rows - b, 0.0))).astype(_BF16)
        level = (ri % (2 * s) >= s) & (ci % (2 * s) < s) & (run == ci - ci % (2 * s))
        total = total + jnp.where(level, lax.dot_general(qs, ks, _NT, preferred_element_type=_F32), 0.0)
        s *= 2
    return total


def _mixer_tile(pcur, st_from, st_to, x_ref, o_ref, mix_ref, gup_ref, gbias_ref, lb_ref, gon_ref, hon_ref, wout_ref,
                *, r0, tm, blk, fill, safe, st_keep=None):
    nb = tm // blk
    rows = pl.ds(r0, tm)
    row = lax.broadcasted_iota(jnp.int32, (tm, tm), 0)
    col = lax.broadcasted_iota(jnp.int32, (tm, tm), 1)
    tri = ((col <= row) & (col >= row - row % blk)).astype(_BF16)
    lane = lax.broadcasted_iota(jnp.int32, (tm, LANES), 1)
    lb = lb_ref[...]

    def cat(x, y, axis):
        return jnp.concatenate([x, y], axis=axis)

    def cat_rows(parts):
        return parts[0] if len(parts) == 1 else jnp.concatenate(parts, axis=0)

    def bdiag(x, y):
        return cat(cat(x, jnp.zeros_like(y), 1), cat(jnp.zeros_like(x), y, 1), 0)

    gd = pcur(_GD, _GD + LANES).astype(_BF16)
    logits = jnp.dot(gd, gup_ref[...], preferred_element_type=_F32) + gbias_ref[...]
    g_cols = [(jnp.minimum(logits, 0.0) - jnp.log1p(jnp.exp(-jnp.abs(logits)))) / GLA_GATE_NORM]
    k_in = []
    for hd in range(HG_HEADS):
        z = pcur(_HF + hd * LANES, _HF + (hd + 1) * LANES)
        lb_h = lb[:, hd * LANES:(hd + 1) * LANES]
        e = jnp.exp(-jnp.abs(z))
        r = 1.0 / (1.0 + e)
        sig_pos = jnp.where(z >= 0, r, e * r)
        sig_neg = jnp.where(z >= 0, e * r, r)
        g_cols.append(jnp.log(lb_h + (1.0 - lb_h) * sig_pos))
        k_in.append((1.0 - lb_h) * sig_neg)
    fill(0)
    b_all = _dot_split_rhs(tri, jnp.concatenate(g_cols, axis=1), 2)
    b_min = b_all[blk - 1:blk, :]
    for c in range(1, nb):
        b_min = jnp.minimum(b_min, b_all[(c + 1) * blk - 1:(c + 1) * blk, :])
    b_min = jnp.min(jnp.broadcast_to(b_min, (SUBLANES, b_min.shape[1])))

    def k_scale(b):
        if not safe:
            return jnp.exp(-b)
        b_last = [jnp.broadcast_to(b[(c + 1) * blk - 1:(c + 1) * blk, :], (blk, LANES)) for c in range(nb)]
        return jnp.exp(cat_rows(b_last) - b)

    qs, ks, bs, qts, kts, vs, ebs = [], [], [], [], [], [], []
    for p in range(GLA_HEADS // 2):
        b = b_all[:, p * LANES:(p + 1) * LANES]
        eb = jnp.exp(b)
        ek = k_scale(b)
        q = pcur(_GQ + p * LANES, _GQ + (p + 1) * LANES) * (GLA_DK ** -0.5)
        k = pcur(_GK + p * LANES, _GK + (p + 1) * LANES)
        for hh in range(2):
            idx = 2 * p + hh
            m = (lane >= hh * GLA_DK) & (lane < (hh + 1) * GLA_DK)
            qs.append(jnp.where(m, q, 0.0))
            ks.append(jnp.where(m, k, 0.0))
            bs.append(b)
            qts.append((qs[-1] * eb).astype(_BF16))
            kts.append((ks[-1] * ek).astype(_BF16))
            vs.append(pcur(_GV + idx * GLA_DV, _GV + (idx + 1) * GLA_DV).astype(_BF16))
            ebs.append(eb)
    for hd in range(HG_HEADS):
        b = b_all[:, (2 + hd) * LANES:(3 + hd) * LANES]
        eb = jnp.exp(b)
        hq = pcur(_HQ + hd * LANES, _HQ + (hd + 1) * LANES)
        qs.append(hq * _sigmoid(hq))
        ks.append(k_in[hd])
        bs.append(b)
        qts.append((qs[-1] * eb).astype(_BF16))
        kts.append((ks[-1] * k_scale(b)).astype(_BF16))
        vs.append(pcur(_HI + hd * LANES, _HI + (hd + 1) * LANES).astype(_BF16))
        ebs.append(eb)

    gs = min(tm, LANES)
    groups = [slice(u * gs, (u + 1) * gs) for u in range(tm // gs)]
    blocks = [slice(c * blk, (c + 1) * blk) for c in range(nb)]
    pairs = [(2 * p, 2 * p + 1) for p in range(N_HEADS // 2)]
    row2 = lax.broadcasted_iota(jnp.int32, (gs, 2 * gs), 0)
    col2 = lax.broadcasted_iota(jnp.int32, (gs, 2 * gs), 1)
    col2 = jnp.where(col2 >= gs, col2 - gs, col2)
    causal2 = (col2 <= row2) & (col2 >= row2 - row2 % blk)

    q2 = [cat(qts[a], qts[b], 1) for a, b in pairs]
    fill(1)
    kvs = [[lax.dot_general(cat(vs[a][rs], vs[b][rs], 0), bdiag(kts[a][rs], kts[b][rs]), _TN,
                            preferred_element_type=_F32)
            for rs in blocks] for a, b in pairs]

    if safe:
        scores = [[cat(_safe_scores(qs[a][g], ks[a][g], bs[a][g], blk),
                       _safe_scores(qs[b][g], ks[b][g], bs[b][g], blk), 1) for g in groups] for a, b in pairs]
    else:
        scores = [[lax.dot_general(q2[p][g], bdiag(kts[a][g], kts[b][g]), _NT, preferred_element_type=_F32)
                   for g in groups] for p, (a, b) in enumerate(pairs)]
    fill(2)
    o_intra = [[jnp.dot(jnp.where(causal2, scores[p][u], 0.0).astype(_BF16), bdiag(vs[a][g], vs[b][g]),
                        preferred_element_type=_F32)
                for u, g in enumerate(groups)] for p, (a, b) in enumerate(pairs)]
    fill(3)
    n_streams = st_from.shape[0] // N_HEADS
    assert n_streams in (1, nb)

    def load_states(stream):
        base = stream * N_HEADS
        loaded = [cat(st_from[base + a], st_from[base + b], 1) for a, b in pairs]
        if st_keep is not None:
            for h in range(N_HEADS):
                st_keep[base + h] = loaded[h // 2][:, (h % 2) * LANES:(h % 2 + 1) * LANES]
        return loaded

    def store_states(stream, values):
        for p, (a, b) in enumerate(pairs):
            st_to[stream * N_HEADS + a] = values[p][:, :LANES]
            st_to[stream * N_HEADS + b] = values[p][:, LANES:]

    states = load_states(0)
    e2 = [cat(ebs[a], ebs[b], 1) for a, b in pairs]
    o_inter = [[] for _ in pairs]
    for c, rs in enumerate(blocks):
        if n_streams > 1 and c > 0:
            states = load_states(c)
        for p in range(len(pairs)):
            s2 = states[p].astype(_BF16)
            o_inter[p].append(lax.dot_general(q2[p][rs], bdiag(s2[:, :LANES], s2[:, LANES:]), _NT,
                                              preferred_element_type=_F32))
        for p in range(len(pairs)):
            e_last = e2[p][(c + 1) * blk - 1:(c + 1) * blk, :]
            states[p] = states[p] * e_last + kvs[p][c] if safe else (states[p] + kvs[p][c]) * e_last
        if n_streams > 1:
            store_states(c, states)
    fill(4)
    if n_streams == 1:
        store_states(0, states)
    for p, (a, b) in enumerate(pairs):
        o2 = cat_rows(o_intra[p]) + cat_rows(o_inter[p])
        for hh, h in enumerate((a, b)):
            o = o2[:, hh * LANES:(hh + 1) * LANES]
            onorm = gon_ref[...] if h < GLA_HEADS else hon_ref[...]
            gate_col = _GR + h * LANES if h < GLA_HEADS else _HGATE + (h - GLA_HEADS) * LANES
            gate = pcur(gate_col, gate_col + LANES)
            o = o * lax.rsqrt(jnp.mean(o * o, axis=-1, keepdims=True) + EPS) * onorm
            o = o * (gate * _sigmoid(gate))
            mix_ref[rows, h * LANES:(h + 1) * LANES] = o.astype(_BF16)
    fill(5)
    out = jnp.dot(mix_ref[rows, :], wout_ref[...], preferred_element_type=_F32)
    fill(6)
    o_ref[rows, :] = x_ref[rows, :] + out
    return b_min


def _mixer_kernel(x_ref, xn_ref, st_in_ref, g_ref, win_ref, gup_ref, gbias_ref, lb_ref, gon_ref, hon_ref, wout_ref,
                  o_ref, st_ref, pa_ref, pb_ref, ha_ref, hb_ref, mix_ref, sk_ref, *, tm, blk, pipelined):
    u = pl.program_id(1)
    tile = functools.partial(_mixer_tile, x_ref=x_ref, o_ref=o_ref, mix_ref=mix_ref, gup_ref=gup_ref,
                             gbias_ref=gbias_ref, lb_ref=lb_ref, gon_ref=gon_ref, hon_ref=hon_ref,
                             wout_ref=wout_ref, tm=tm, blk=blk)
    no_fill = lambda stage: None

    def normed(src_ref, r0):
        return _rmsnorm(src_ref[pl.ds(r0, tm), :], g_ref[...]).astype(_BF16)

    valid = {id(pa_ref): set(range(N_PROJ_PIECES)), id(pb_ref): set()}

    def project(h, dst_ref, pieces, for_reader):
        for i in pieces:
            cols = slice(i * PROJ_PIECE, min((i + 1) * PROJ_PIECE, D_PROJ))
            dst_ref[:, cols] = jnp.dot(h, win_ref[:, cols], preferred_element_type=_F32)
            (valid[id(dst_ref)].add if for_reader else valid[id(dst_ref)].discard)(i)

    def reader(buf_ref):
        def read(c0, c1):
            assert all(i in valid[id(buf_ref)] for i in range(c0 // PROJ_PIECE, (c1 - 1) // PROJ_PIECE + 1))
            return buf_ref[:, c0:c1]
        return read

    def filler(h_ref, dst_ref, for_reader):
        pieces = iter(range(N_PROJ_PIECES))
        return lambda stage: project(h_ref[...], dst_ref, [next(pieces) for _ in range(PROJ_SCHEDULE[stage])],
                                     for_reader)

    @pl.when(u == 0)
    def _():
        for h in range(st_ref.shape[0]):
            st_ref[h] = st_in_ref[h].T

    if pipelined:
        @pl.when((pl.program_id(0) == 0) & (u == 0))
        def _():
            project(normed(x_ref, 0), pa_ref, range(N_PROJ_PIECES), True)
            hb_ref[...] = normed(x_ref, tm)

        ha_ref[...] = normed(xn_ref, 0)
        low_a = tile(reader(pa_ref), st_ref, sk_ref.at[1], r0=0, fill=filler(hb_ref, pb_ref, True), safe=False,
                     st_keep=sk_ref.at[0])
        low_b = tile(reader(pb_ref), sk_ref.at[1], st_ref, r0=tm, fill=filler(ha_ref, pa_ref, False), safe=False)
        hb_ref[...] = normed(xn_ref, tm)
        low_ab = jnp.minimum(low_a, low_b)

        def redo(i, carry):
            @pl.when(jnp.where(i == 0, low_a, low_ab) < -DECAY_GUARD)
            def _():
                r0 = pl.multiple_of(i * tm, tm)
                project(normed(x_ref, r0), pb_ref, range(N_PROJ_PIECES), True)
                tile(reader(pb_ref), sk_ref.at[i], sk_ref.at[i + 1], r0=r0, fill=no_fill, safe=True)
            return carry

        lax.fori_loop(0, 2, redo, 0)

        @pl.when(low_ab < -DECAY_GUARD)
        def _():
            for h in range(N_HEADS):
                st_ref[h] = sk_ref[2, h]
    else:
        project(normed(x_ref, 0), pa_ref, range(N_PROJ_PIECES), True)
        low_a = tile(reader(pa_ref), st_ref, st_ref, r0=0, fill=no_fill, safe=False, st_keep=sk_ref.at[0])

        @pl.when(low_a < -DECAY_GUARD)
        def _():
            tile(reader(pa_ref), sk_ref.at[0], st_ref, r0=0, fill=no_fill, safe=True)


    @pl.when(u == pl.num_programs(1) - 1)
    def _():
        for h in range(st_ref.shape[0]):
            st_ref[h] = st_ref[h].T


def _mixer(x, st_in, g, win, gup, gbias, lb, gon, hon, wout, *, tm, blk):
    bsz, seq, _ = x.shape
    pipelined = seq % (2 * tm) == 0
    rows = 2 * tm if pipelined else tm
    assert seq % rows == 0 and tm % blk == 0 and blk & (blk - 1) == 0
    nu = seq // rows
    last = bsz * nu - 1

    def next_block(b, u):
        nxt = jnp.minimum(b * nu + u + 1, last)
        return (nxt // nu, nxt % nu, 0)

    row_spec = pl.BlockSpec((None, rows, D_MODEL), lambda b, u: (b, u, 0))
    n_states = st_in.shape[1]
    assert n_states == N_HEADS or (not pipelined and n_states == (tm // blk) * N_HEADS)
    st_spec = pl.BlockSpec((None, n_states, LANES, LANES), lambda b, u: (b, 0, 0, 0))
    return pl.pallas_call(
        functools.partial(_mixer_kernel, tm=tm, blk=blk, pipelined=pipelined),
        grid=(bsz, nu),
        in_specs=[row_spec, pl.BlockSpec((None, rows, D_MODEL), next_block), st_spec]
        + [_const_spec(a.shape) for a in (g, win, gup, gbias, lb, gon, hon, wout)],
        out_specs=[row_spec, st_spec],
        out_shape=[jax.ShapeDtypeStruct(x.shape, _F32),
                   jax.ShapeDtypeStruct((bsz, n_states, LANES, LANES), _F32)],
        scratch_shapes=[pltpu.VMEM((tm, D_PROJ), _F32), pltpu.VMEM((tm, D_PROJ), _F32),
                        pltpu.VMEM((tm, D_MODEL), _BF16), pltpu.VMEM((tm, D_MODEL), _BF16),
                        pltpu.VMEM((rows, D_MODEL), _BF16),
                        pltpu.VMEM((3 if pipelined else 1, n_states, LANES, LANES), _F32)],
        compiler_params=pltpu.CompilerParams(dimension_semantics=("arbitrary", "arbitrary"),
                                             vmem_limit_bytes=VMEM_LIMIT),
        name="mixer",
    )(x, x, st_in, g, win, gup, gbias, lb, gon, hon, wout)


_W_IN_GROUPS = (("gq", GLA_HEADS * GLA_DK, _GQ), ("gk", GLA_HEADS * GLA_DK, _GK), ("gv", GLA_HEADS * GLA_DV, _GV),
                ("gr", GLA_HEADS * GLA_DV, _GR), ("gd", GLA_GATE_RANK, _GD), ("hq", HG_HEADS * HG_EXPAND, _HQ),
                ("hf", HG_HEADS * HG_EXPAND, _HF), ("hi", HG_HEADS * HG_HEAD_DIM, _HI),
                ("hgate", HG_HEADS * HG_HEAD_DIM, _HGATE))
W_IN_CHUNK = 256


def _w_in_kernel(wt_ref, o_ref):
    src = 0
    for name, width, dst in _W_IN_GROUPS:
        if name == "gd":
            slab = wt_ref[src:src + LANES, :].T
            keep = lax.broadcasted_iota(jnp.int32, slab.shape, 1) < width
            o_ref[:, dst:dst + LANES] = jnp.where(keep, slab, 0.0).astype(_BF16)
            o_ref[:, dst + LANES:_HF] = jnp.zeros((D_MODEL, _HF - dst - LANES), _BF16)
        else:
            for c in range(0, width, W_IN_CHUNK):
                o_ref[:, dst + c:dst + c + W_IN_CHUNK] = wt_ref[src + c:src + c + W_IN_CHUNK, :].T.astype(_BF16)
        src += width


def _reorder_w_in(wt):
    d_in = sum(width for _, width, _ in _W_IN_GROUPS)
    assert wt.shape == (d_in, D_MODEL)
    return pl.pallas_call(
        _w_in_kernel,
        grid=(1,),
        in_specs=[_const_spec(wt.shape)],
        out_specs=pl.BlockSpec((D_MODEL, D_PROJ), lambda i: (0, 0)),
        out_shape=jax.ShapeDtypeStruct((D_MODEL, D_PROJ), _BF16),
        compiler_params=pltpu.CompilerParams(vmem_limit_bytes=VMEM_LIMIT),
        name="w_in_order",
    )(wt)


def _pack_state(s_gla, s_hg):
    zeros = jnp.zeros_like(s_gla)
    even = jnp.concatenate([s_gla, zeros], axis=2)
    odd = jnp.concatenate([zeros, s_gla], axis=2)
    is_odd = (jnp.arange(GLA_HEADS) % 2 == 1)[None, :, None, None]
    gla = jnp.where(is_odd, odd, even)
    return jnp.concatenate([gla, s_hg], axis=1).astype(_F32)


def _unpack_state(st):
    gla = st[:, :GLA_HEADS]
    is_odd = (jnp.arange(GLA_HEADS) % 2 == 1)[None, :, None, None]
    gla = jnp.where(is_odd, gla[:, :, GLA_DK:], gla[:, :, :GLA_DK])
    return gla, st[:, GLA_HEADS:]


def kernel(x_prompt, x_sample, state_gla, state_hgrn, norm_ffn1, ffn1_w_gate, ffn1_w_up, ffn1_w_down, norm_mix, w_in, gla_gate_up, gla_gate_bias, gla_onorm, hg_lower_bound_logits, hg_onorm, w_out, norm_ffn2, ffn2_w_gate, ffn2_w_up, ffn2_w_down, norm_final):
    assert norm_ffn1.shape[0] == 1, "single layer"
    row = lambda v: v.reshape(1, -1).astype(_F32)
    bf = lambda w: w.astype(_BF16)
    f1 = (row(norm_ffn1[0]), ffn1_w_gate[0], ffn1_w_up[0], ffn1_w_down[0], row(norm_final))
    f2 = (row(norm_ffn2[0]), ffn2_w_gate[0], ffn2_w_up[0], ffn2_w_down[0], row(norm_final))
    win = _reorder_w_in(w_in[0].T)
    gup = jnp.concatenate([gla_gate_up[0], jnp.zeros((LANES - GLA_GATE_RANK, GLA_HEADS * GLA_DK), _F32)],
                          axis=0).astype(_BF16)
    lb = jnp.cumsum(jax.nn.softmax(hg_lower_bound_logits.astype(_F32), axis=0), axis=0)[0]
    mix_w = (row(norm_mix[0]), win, gup, row(gla_gate_bias[0]), row(lb), row(gla_onorm[0]), row(hg_onorm[0]),
             bf(w_out[0]))

    def trunk(x, s_gla, s_hg, tm_ffn, tm_mix, blk):
        bsz, seq, _ = x.shape
        x1 = _ffn(x.reshape(bsz * seq, D_MODEL), *f1, tm=tm_ffn, final_norm=False)
        st = _pack_state(s_gla, s_hg)
        if seq == blk and (bsz * seq) % LANES == 0 and bsz * seq <= tm_mix:
            x2, st = _mixer(x1.reshape(1, bsz * seq, D_MODEL), st.reshape(1, bsz * N_HEADS, LANES, LANES), *mix_w,
                            tm=bsz * seq, blk=blk)
            st = st.reshape(bsz, N_HEADS, LANES, LANES)
        else:
            x2, st = _mixer(x1.reshape(bsz, seq, D_MODEL), st, *mix_w, tm=min(tm_mix, seq), blk=blk)
        y = _ffn(x2.reshape(bsz * seq, D_MODEL), *f2, tm=tm_ffn, final_norm=True)
        g, h = _unpack_state(st)
        return y.reshape(bsz, seq, D_MODEL), g[None], h[None]

    bp, sp, _ = x_prompt.shape
    bs, ss, _ = x_sample.shape
    zg = jnp.zeros((bp, GLA_HEADS, GLA_DK, GLA_DV), _F32)
    zh = jnp.zeros((bp, HG_HEADS, HG_EXPAND, HG_HEAD_DIM), _F32)
    blk_p = min(CHUNK, sp)
    y_p, g_p, h_p = trunk(x_prompt, zg, zh, FFN_ROWS, MIX_ROWS, blk_p)
    y_s, g_s, h_s = trunk(x_sample, state_gla[0], state_hgrn[0], bs * ss, MIX_ROWS, ss)
    sd_g, sd_h = state_gla.dtype, state_hgrn.dtype
    return (y_p, y_s, g_p.astype(sd_g), h_p.astype(sd_h), g_s.astype(sd_g), h_s.astype(sd_h))
```
